```python
import math
import jax, jax.numpy as jnp
from jax import lax
import numpy as np

D_MODEL = 1024
BATCH = 2
SEQ = 8192
DEPTH = 1

D_MIX = D_MODEL
HEAD_DIM = 64
D_ATTN = D_MIX // 2
D_CONV = D_MIX - D_ATTN
N_HEADS = D_ATTN // HEAD_DIM
N_KV_HEADS = 2
GQA_GROUP = N_HEADS // N_KV_HEADS
D_KV = N_KV_HEADS * HEAD_DIM
WINDOW = 128
BLOCK = WINDOW
CONV_WIDTH = 31
N_BUCKETS = 32
MAX_DISTANCE = 128
D_FF = 2816
FFN_CONV_WIDTH = 3
LN_EPS = 1e-5
DEEPNORM_ALPHA = (2.0 * DEPTH) ** 0.25
DEEPNORM_BETA = (8.0 * DEPTH) ** -0.25

Q_END = D_ATTN
K_END = Q_END + D_KV
V_END = K_END + D_KV
A_END = V_END + D_CONV
D_IN = A_END + D_CONV

kernel_name = "hybrid_swa_sink_conformer_convffn_deepnorm"


def layer_norm(x, g, b):
    xf = x.astype(jnp.float32)
    mu = jnp.mean(xf, axis=-1, keepdims=True)
    var = jnp.mean(jnp.square(xf - mu), axis=-1, keepdims=True)
    y = (xf - mu) * lax.rsqrt(var + LN_EPS)
    return (y * g.astype(jnp.float32) + b.astype(jnp.float32)).astype(x.dtype)


def rms_norm(x, g):
    xf = x.astype(jnp.float32)
    y = xf * lax.rsqrt(jnp.mean(jnp.square(xf), axis=-1, keepdims=True) + LN_EPS)
    return (y * g.astype(jnp.float32)).astype(x.dtype)


def causal_depthwise_conv(x, w, b):
    k, c = w.shape
    y = lax.conv_general_dilated(
        x, w[:, None, :].astype(x.dtype), window_strides=(1,), padding=[(k - 1, 0)],
        dimension_numbers=("NWC", "WIO", "NWC"), feature_group_count=c)
    return y + b


def t5_causal_bucket(n):
    max_exact = N_BUCKETS // 2
    nf = jnp.maximum(n, max_exact).astype(jnp.float32)
    large = max_exact + (jnp.log(nf / max_exact) / math.log(MAX_DISTANCE / max_exact)
                         * (N_BUCKETS - max_exact)).astype(jnp.int32)
    large = jnp.minimum(large, N_BUCKETS - 1)
    return jnp.where(n < max_exact, n, large)


def sliding_window_gqa(q, k, v, sinks, rel_bias_table):
    b, s = q.shape[:2]
    nb = s // BLOCK
    qb = q.reshape(b, nb, BLOCK, N_KV_HEADS, GQA_GROUP, HEAD_DIM)

    def band(t):
        t = t.reshape(b, s, N_KV_HEADS, HEAD_DIM)
        tp = jnp.pad(t, ((0, 0), (BLOCK, 0), (0, 0), (0, 0)))
        tp = tp.reshape(b, nb + 1, BLOCK, N_KV_HEADS, HEAD_DIM)
        return jnp.concatenate([tp[:, :-1], tp[:, 1:]], axis=2)

    kb, vb = band(k), band(v)
    scale = HEAD_DIM ** -0.5
    scores = jnp.einsum("bnqhgd,bnkhd->bhgnqk", qb, kb).astype(jnp.float32) * scale

    qi = jnp.arange(BLOCK)[:, None]
    kj = jnp.arange(2 * BLOCK)[None, :]
    dist = qi + BLOCK - kj
    band_ok = (dist >= 0) & (dist < WINDOW)
    bias = rel_bias_table[t5_causal_bucket(jnp.maximum(dist, 0))]
    bias = bias.astype(jnp.float32).transpose(2, 0, 1).reshape(
        N_KV_HEADS, GQA_GROUP, BLOCK, 2 * BLOCK)
    key_pos = jnp.arange(nb)[:, None] * BLOCK - BLOCK + jnp.arange(2 * BLOCK)[None, :]
    mask = band_ok[None] & (key_pos >= 0)[:, None, :]

    scores = jnp.where(mask, scores + bias[:, :, None], -jnp.inf)
    sink = sinks.astype(jnp.float32).reshape(N_KV_HEADS, GQA_GROUP)[None, :, :, None, None, None]
    m = jnp.maximum(jnp.max(scores, axis=-1, keepdims=True), sink)
    p = jnp.exp(scores - m)
    denom = jnp.sum(p, axis=-1, keepdims=True) + jnp.exp(sink - m)
    probs = (p / denom).astype(v.dtype)
    out = jnp.einsum("bhgnqk,bnkhd->bnqhgd", probs, vb)
    return out.reshape(b, s, D_ATTN)


def conformer_conv_group(a, gate, dw_w, dw_b, ln_g, ln_b):
    h = a * jax.nn.sigmoid(gate)
    h = causal_depthwise_conv(h, dw_w, dw_b)
    h = layer_norm(h, ln_g, ln_b)
    return jax.nn.silu(h)


def hybrid_mixer(x, w_in, b_in, sinks, rel_bias_table, conv_dw_w, conv_dw_b,
                 conv_ln_g, conv_ln_b, attn_out_gain, conv_out_gain, w_out, b_out):
    proj = x @ w_in + b_in
    q, k, v, a, gate = jnp.split(proj, [Q_END, K_END, V_END, A_END], axis=-1)
    y_attn = sliding_window_gqa(q, k, v, sinks, rel_bias_table)
    y_conv = conformer_conv_group(a, gate, conv_dw_w, conv_dw_b, conv_ln_g, conv_ln_b)
    y = jnp.concatenate([rms_norm(y_attn, attn_out_gain),
                         rms_norm(y_conv, conv_out_gain)], axis=-1)
    return y @ w_out + b_out


def conv_ffn(x, w_up, dw_w, dw_b, w_down):
    h = causal_depthwise_conv(x @ w_up, dw_w, dw_b)
    g, u = jnp.split(h, 2, axis=-1)
    return (jax.nn.silu(g) * u) @ w_down


def setup_inputs(seed: int = 0) -> dict:
    key = jax.random.key(seed)
    ks = jax.random.split(key, 24)
    f32 = jnp.float32
    L = DEPTH

    def nrm(k, shape, scale):
        return jax.random.normal(k, shape, f32) * scale

    return {
        "x": nrm(ks[0], (BATCH, SEQ, D_MODEL), 1.0),
        "w_in": nrm(ks[1], (L, D_MODEL, D_IN), D_MODEL ** -0.5),
        "b_in": nrm(ks[2], (L, D_IN), 0.02),
        "attn_sinks": nrm(ks[3], (L, N_HEADS), 1.0),
        "rel_bias_table": nrm(ks[4], (N_BUCKETS, N_HEADS), 0.5),
        "conv_dw_w": nrm(ks[5], (L, CONV_WIDTH, D_CONV), CONV_WIDTH ** -0.5),
        "conv_dw_b": nrm(ks[6], (L, D_CONV), 0.02),
        "conv_ln_g": 1.0 + nrm(ks[7], (L, D_CONV), 0.02),
        "conv_ln_b": nrm(ks[8], (L, D_CONV), 0.02),
        "attn_out_gain": 1.0 + nrm(ks[9], (L, D_ATTN), 0.02),
        "conv_out_gain": 1.0 + nrm(ks[10], (L, D_CONV), 0.02),
        "w_out": nrm(ks[11], (L, D_MIX, D_MODEL), D_MIX ** -0.5 * DEEPNORM_BETA),
        "b_out": nrm(ks[12], (L, D_MODEL), 0.02),
        "ln1_g": 1.0 + nrm(ks[13], (L, D_MODEL), 0.02),
        "ln1_b": nrm(ks[14], (L, D_MODEL), 0.02),
        "w_up": nrm(ks[15], (L, D_MODEL, 2 * D_FF), D_MODEL ** -0.5),
        "ffn_dw_w": nrm(ks[16], (L, FFN_CONV_WIDTH, 2 * D_FF), FFN_CONV_WIDTH ** -0.5),
        "ffn_dw_b": nrm(ks[17], (L, 2 * D_FF), 0.02),
        "w_down": nrm(ks[18], (L, D_FF, D_MODEL), D_FF ** -0.5 * DEEPNORM_BETA),
        "ln2_g": 1.0 + nrm(ks[19], (L, D_MODEL), 0.02),
        "ln2_b": nrm(ks[20], (L, D_MODEL), 0.02),
    }


def reference(x, w_in, b_in, attn_sinks, rel_bias_table, conv_dw_w, conv_dw_b,
              conv_ln_g, conv_ln_b, attn_out_gain, conv_out_gain, w_out, b_out,
              ln1_g, ln1_b, w_up, ffn_dw_w, ffn_dw_b, w_down, ln2_g, ln2_b):
    for l in range(DEPTH):
        mix = hybrid_mixer(x, w_in[l], b_in[l], attn_sinks[l], rel_bias_table,
                           conv_dw_w[l], conv_dw_b[l], conv_ln_g[l], conv_ln_b[l],
                           attn_out_gain[l], conv_out_gain[l], w_out[l], b_out[l])
        x = layer_norm(DEEPNORM_ALPHA * x + mix, ln1_g[l], ln1_b[l])
        ffn = conv_ffn(x, w_up[l], ffn_dw_w[l], ffn_dw_b[l], w_down[l])
        x = layer_norm(DEEPNORM_ALPHA * x + ffn, ln2_g[l], ln2_b[l])
    return x
```

```python
import functools
import math

import jax
import jax.numpy as jnp
from jax import lax
from jax.experimental import pallas as pl
from jax.experimental.pallas import tpu as pltpu

HEAD_DIM = 64
N_KV_HEADS = 2
GQA_GROUP = 4
N_HEADS = N_KV_HEADS * GQA_GROUP
WINDOW = 128
CONV_WIDTH = 31
N_BUCKETS = 32
MAX_DISTANCE = 128
FFN_CONV_WIDTH = 3
LN_EPS = 1e-5

SEQ_TILE = 512
CONV_HALO = 32
CONV_ROWS = 32
FFN_CHUNK = 256
FFN_HALO = 8
VMEM_LIMIT_BYTES = 56 * 1024 * 1024

BF16 = jnp.bfloat16
F32 = jnp.float32


def _layer_norm(r, g, b):
    mu = jnp.mean(r, axis=-1, keepdims=True)
    d = r - mu
    var = jnp.mean(d * d, axis=-1, keepdims=True)
    return d * lax.rsqrt(var + LN_EPS) * g + b


def _rms_norm(y, g):
    return y * lax.rsqrt(jnp.mean(y * y, axis=-1, keepdims=True) + LN_EPS) * g


def _dot_nt(a, b):
    return lax.dot_general(a, b, (((1,), (1,)), ((), ())), preferred_element_type=F32)


def _layer_kernel(alpha,
                  x_ref, w_in_ref, b_in_ref, sinks_ref, tbl_ref, cw_ref, cb_ref, clg_ref, clb_ref,
                  ag_ref, cg_ref, w_out_ref, b_out_ref, l1g_ref, l1b_ref,
                  wug_ref, wuu_ref, fwg_ref, fwu_ref, fbg_ref, fbu_ref, wd_ref, l2g_ref, l2b_ref,
                  o_ref,
                  bias_ref, kv_ref, hbuf_ref, hshift_ref, ya_ref, y_ref, x1_ref, x1b_ref, acc_ref,
                  ug_ref, uu_ref, cg_carry_ref, cu_carry_ref):
    ts = x_ref.shape[0]
    d_attn = N_HEADS * HEAD_DIM
    d_kv = N_KV_HEADS * HEAD_DIM
    d_conv = cw_ref.shape[1]
    n_chunks = wd_ref.shape[0]
    b = pl.program_id(0)
    i = pl.program_id(1)

    qi = lax.broadcasted_iota(jnp.int32, (WINDOW, WINDOW), 0)
    kj = lax.broadcasted_iota(jnp.int32, (WINDOW, WINDOW), 1)
    tri = kj <= qi

    @pl.when((b == 0) & (i == 0))
    def _build_bias():
        n = (qi - kj) & (WINDOW - 1)
        max_exact = N_BUCKETS // 2
        nf = jnp.maximum(n, max_exact).astype(F32)
        large = max_exact + (jnp.log(nf / max_exact) / math.log(MAX_DISTANCE / max_exact)
                             * (N_BUCKETS - max_exact)).astype(jnp.int32)
        large = jnp.minimum(large, N_BUCKETS - 1)
        bucket = jnp.where(n < max_exact, n, large)
        for hd in range(N_HEADS):
            bias = jnp.zeros((WINDOW, WINDOW), F32)
            for bkt in range(N_BUCKETS):
                bias = jnp.where(bucket == bkt, tbl_ref[bkt * N_HEADS + hd], bias)
            bias_ref[hd] = bias

    @pl.when(i == 0)
    def _reset_history():
        kv_ref[:, 0:WINDOW, :] = jnp.zeros((8, WINDOW, d_kv), BF16)
        hbuf_ref[0:CONV_HALO, :] = jnp.zeros((CONV_HALO, d_conv), F32)
        cg_carry_ref[...] = jnp.zeros(cg_carry_ref.shape, F32)
        cu_carry_ref[...] = jnp.zeros(cu_carry_ref.shape, F32)

    x = x_ref[...]
    proj = jnp.dot(x.astype(BF16), w_in_ref[...], preferred_element_type=F32) + b_in_ref[...]
    q_end, k_end, v_end, a_end = d_attn, d_attn + d_kv, d_attn + 2 * d_kv, d_attn + 2 * d_kv + d_conv

    lane = lax.broadcasted_iota(jnp.int32, (ts, d_kv), 1)
    lo = lane < HEAD_DIM
    for base, t in ((0, proj[:, q_end:k_end]), (4, proj[:, k_end:v_end])):
        tr = pltpu.roll(t, HEAD_DIM, axis=1)
        zero = jnp.zeros_like(t)
        kv_ref[base + 0, WINDOW:WINDOW + ts, :] = jnp.where(lo, t, zero).astype(BF16)
        kv_ref[base + 1, WINDOW:WINDOW + ts, :] = jnp.where(lo, zero, tr).astype(BF16)
        kv_ref[base + 2, WINDOW:WINDOW + ts, :] = jnp.where(lo, tr, zero).astype(BF16)
        kv_ref[base + 3, WINDOW:WINDOW + ts, :] = jnp.where(lo, zero, t).astype(BF16)

    qs = (proj[:, 0:q_end] * (HEAD_DIM ** -0.5)).astype(BF16)
    lo_w = lax.broadcasted_iota(jnp.int32, (WINDOW, 2 * HEAD_DIM), 1) < HEAD_DIM
    for j in range(ts // WINDOW):
        r0 = j * WINDOW
        has_prev = (i > 0) | (j > 0)
        prev_mask = jnp.where(has_prev, 0.0, -jnp.inf).astype(F32)
        qb = qs[r0:r0 + WINDOW, :]
        pair_out = []
        for h in range(N_KV_HEADS):
            q2 = jnp.concatenate([qb[:, (2 * h) * 128:(2 * h + 1) * 128],
                                  qb[:, (2 * h + 1) * 128:(2 * h + 2) * 128]], axis=0)
            probs = {}
            inv = {}
            for ab in range(2):
                keys = kv_ref[2 * h + ab, r0:r0 + 2 * WINDOW, :]
                s = _dot_nt(q2, keys)
                for pp in range(2):
                    hd = 4 * h + 2 * pp + ab
                    s_prev = s[pp * WINDOW:(pp + 1) * WINDOW, 0:WINDOW]
                    s_cur = s[pp * WINDOW:(pp + 1) * WINDOW, WINDOW:2 * WINDOW]
                    sc = jnp.where(tri, s_cur, s_prev + prev_mask) + bias_ref[hd]
                    sink = sinks_ref[hd]
                    m = jnp.maximum(jnp.max(sc, axis=-1, keepdims=True), sink)
                    p = jnp.exp(sc - m)
                    denom = jnp.sum(p, axis=-1, keepdims=True) + jnp.exp(sink - m)
                    zero = jnp.zeros_like(p)
                    probs[(pp, ab)] = jnp.concatenate(
                        [jnp.where(tri, zero, p), jnp.where(tri, p, zero)], axis=1).astype(BF16)
                    inv[(pp, ab)] = 1.0 / denom
            for pp in range(2):
                va = kv_ref[4 + 2 * h + 0, r0:r0 + 2 * WINDOW, :]
                vb = kv_ref[4 + 2 * h + 1, r0:r0 + 2 * WINDOW, :]
                o = (jnp.dot(probs[(pp, 0)], va, preferred_element_type=F32)
                     + jnp.dot(probs[(pp, 1)], vb, preferred_element_type=F32))
                pair_out.append(o * jnp.where(lo_w, inv[(pp, 0)], inv[(pp, 1)]))
        ya_ref[r0:r0 + WINDOW, :] = jnp.concatenate(pair_out, axis=1)
    kv_ref[:, 0:WINDOW, :] = kv_ref[:, ts:ts + WINDOW, :]

    y_ref[:, 0:d_attn] = _rms_norm(ya_ref[...], ag_ref[...]).astype(BF16)

    hbuf_ref[CONV_HALO:CONV_HALO + ts, :] = proj[:, v_end:a_end] * jax.nn.sigmoid(proj[:, a_end:])
    for s in range(1, 8):
        hshift_ref[s - 1] = hbuf_ref[s:s + hshift_ref.shape[1], :]

    def conv_chunk(c, carry):
        r0 = pl.multiple_of(c * CONV_ROWS, CONV_ROWS)
        acc = jnp.broadcast_to(cb_ref[...], (CONV_ROWS, d_conv))
        for k in range(CONV_WIDTH):
            off = CONV_HALO - (CONV_WIDTH - 1) + k
            s, aligned = off % 8, off - off % 8
            rows = pl.ds(r0 + aligned, CONV_ROWS)
            tap = hbuf_ref[rows, :] if s == 0 else hshift_ref[s - 1, rows, :]
            acc = acc + cw_ref[k:k + 1, :] * tap
        hn = _layer_norm(acc, clg_ref[...], clb_ref[...])
        hs = hn * jax.nn.sigmoid(hn)
        y_ref[pl.ds(r0, CONV_ROWS), d_attn:d_attn + d_conv] = _rms_norm(hs, cg_ref[...]).astype(BF16)
        return carry

    lax.fori_loop(0, ts // CONV_ROWS, conv_chunk, 0)
    hbuf_ref[0:CONV_HALO, :] = hbuf_ref[ts:ts + CONV_HALO, :]

    mix = jnp.dot(y_ref[...], w_out_ref[...], preferred_element_type=F32) + b_out_ref[...]
    x1 = _layer_norm(alpha * x + mix, l1g_ref[...], l1b_ref[...])
    x1_ref[...] = x1
    x1b_ref[...] = x1.astype(BF16)

    acc_ref[...] = jnp.zeros(acc_ref.shape, F32)
    lo_row = FFN_HALO - (FFN_CONV_WIDTH - 1)

    def ffn_chunk(c, carry):
        x1b = x1b_ref[...]

        def conv3(w_up_ref, fw_ref, fb_ref, buf_ref, carry_ref):
            up = jnp.dot(x1b, w_up_ref[c], preferred_element_type=F32)
            buf_ref[0:FFN_HALO, :] = carry_ref[c]
            buf_ref[FFN_HALO:FFN_HALO + ts, :] = up
            carry_ref[c] = buf_ref[ts:ts + FFN_HALO, :]
            fw = fw_ref[c]
            out = fb_ref[c] + fw[FFN_CONV_WIDTH - 1:FFN_CONV_WIDTH, :] * up
            for k in range(FFN_CONV_WIDTH - 1):
                out = out + fw[k:k + 1, :] * buf_ref[lo_row + k:lo_row + k + ts, :]
            return out

        hg = conv3(wug_ref, fwg_ref, fbg_ref, ug_ref, cg_carry_ref)
        hu = conv3(wuu_ref, fwu_ref, fbu_ref, uu_ref, cu_carry_ref)
        act = (hg * jax.nn.sigmoid(hg) * hu).astype(BF16)
        acc_ref[...] += jnp.dot(act, wd_ref[c], preferred_element_type=F32)
        return carry

    lax.fori_loop(0, n_chunks, ffn_chunk, 0)

    o_ref[...] = _layer_norm(alpha * x1_ref[...] + acc_ref[...], l2g_ref[...], l2b_ref[...])


def _full_spec(shape, single_buffer=False):
    index_map = lambda b, i: (0,) * len(shape)
    if single_buffer:
        return pl.BlockSpec(shape, index_map, pipeline_mode=pl.Buffered(1))
    return pl.BlockSpec(shape, index_map)


def _layer(x, alpha, w_in, b_in, sinks, tbl, cw, cb, clg, clb, ag, cg, w_out, b_out, l1g, l1b,
           w_up, fw, fb, w_down, l2g, l2b, seq_tile):
    bsz, seq, d_model = x.shape
    d_conv = cw.shape[1]
    d_ff = w_down.shape[0]
    d_kv = N_KV_HEADS * HEAD_DIM
    n_chunks = d_ff // FFN_CHUNK
    assert seq % seq_tile == 0 and seq_tile % WINDOW == 0 and d_ff % FFN_CHUNK == 0

    row = lambda v: v.reshape(1, -1).astype(F32)

    def chunked_cols(m):
        return m.reshape(m.shape[0], n_chunks, FFN_CHUNK).transpose(1, 0, 2)

    args = [
        x,
        w_in.astype(BF16), row(b_in), sinks.astype(F32), tbl.reshape(-1).astype(F32),
        cw.astype(F32), row(cb), row(clg), row(clb), row(ag), row(cg),
        w_out.astype(BF16), row(b_out), row(l1g), row(l1b),
        chunked_cols(w_up[:, :d_ff]).astype(BF16), chunked_cols(w_up[:, d_ff:]).astype(BF16),
        chunked_cols(fw[:, :d_ff]).astype(F32), chunked_cols(fw[:, d_ff:]).astype(F32),
        chunked_cols(fb[None, :d_ff]).astype(F32), chunked_cols(fb[None, d_ff:]).astype(F32),
        w_down.reshape(n_chunks, FFN_CHUNK, d_model).astype(BF16), row(l2g), row(l2b),
    ]
    smem = pl.BlockSpec(memory_space=pltpu.SMEM)
    big = {1, 11, 15, 16, 21}
    in_specs = [pl.BlockSpec((None, seq_tile, d_model), lambda b, i: (b, i, 0))]
    for idx, a in enumerate(args[1:], start=1):
        in_specs.append(smem if idx in (3, 4) else _full_spec(a.shape, single_buffer=idx in big))

    scratch = [
        pltpu.VMEM((N_HEADS, WINDOW, WINDOW), F32),
        pltpu.VMEM((8, seq_tile + WINDOW, d_kv), BF16),
        pltpu.VMEM((seq_tile + CONV_HALO, d_conv), F32),
        pltpu.VMEM((7, seq_tile + CONV_HALO - 8, d_conv), F32),
        pltpu.VMEM((seq_tile, N_HEADS * HEAD_DIM), F32),
        pltpu.VMEM((seq_tile, d_model), BF16),
        pltpu.VMEM((seq_tile, d_model), F32),
        pltpu.VMEM((seq_tile, d_model), BF16),
        pltpu.VMEM((seq_tile, d_model), F32),
        pltpu.VMEM((seq_tile + FFN_HALO, FFN_CHUNK), F32),
        pltpu.VMEM((seq_tile + FFN_HALO, FFN_CHUNK), F32),
        pltpu.VMEM((n_chunks, FFN_HALO, FFN_CHUNK), F32),
        pltpu.VMEM((n_chunks, FFN_HALO, FFN_CHUNK), F32),
    ]
    return pl.pallas_call(
        functools.partial(_layer_kernel, alpha),
        grid=(bsz, seq // seq_tile),
        in_specs=in_specs,
        out_specs=pl.BlockSpec((None, seq_tile, d_model), lambda b, i: (b, i, 0)),
        out_shape=jax.ShapeDtypeStruct(x.shape, x.dtype),
        scratch_shapes=scratch,
        compiler_params=pltpu.CompilerParams(
            dimension_semantics=("arbitrary", "arbitrary"),
            vmem_limit_bytes=VMEM_LIMIT_BYTES),
        name="hybrid_layer",
    )(*args)


def kernel(x, w_in, b_in, attn_sinks, rel_bias_table, conv_dw_w, conv_dw_b, conv_ln_g, conv_ln_b,
           attn_out_gain, conv_out_gain, w_out, b_out, ln1_g, ln1_b, w_up, ffn_dw_w, ffn_dw_b, w_down,
           ln2_g, ln2_b, seq_tile=SEQ_TILE):
    depth = w_in.shape[0]
    alpha = (2.0 * depth) ** 0.25
    for l in range(depth):
        x = _layer(x, alpha, w_in[l], b_in[l], attn_sinks[l], rel_bias_table, conv_dw_w[l], conv_dw_b[l],
                   conv_ln_g[l], conv_ln_b[l], attn_out_gain[l], conv_out_gain[l], w_out[l], b_out[l],
                   ln1_g[l], ln1_b[l], w_up[l], ffn_dw_w[l], ffn_dw_b[l], w_down[l], ln2_g[l], ln2_b[l],
                   seq_tile)
    return x
```

```python
import functools
import math

import jax
import jax.numpy as jnp
from jax import lax
from jax.experimental import pallas as pl
from jax.experimental.pallas import tpu as pltpu

HEAD_DIM = 64
N_KV_HEADS = 2
GQA_GROUP = 4
N_HEADS = N_KV_HEADS * GQA_GROUP
WINDOW = 128
CONV_WIDTH = 31
N_BUCKETS = 32
MAX_DISTANCE = 128
FFN_CONV_WIDTH = 3
LN_EPS = 1e-5

SEQ_TILE = 512
CONV_HALO = 32
CONV_ROWS = 32
FFN_CHUNK = 256
LANES = 128
VMEM_LIMIT_BYTES = 56 * 1024 * 1024

BF16 = jnp.bfloat16
F32 = jnp.float32


def _layer_norm(r, g, b):
    mu = jnp.mean(r, axis=-1, keepdims=True)
    d = r - mu
    var = jnp.mean(d * d, axis=-1, keepdims=True)
    return d * lax.rsqrt(var + LN_EPS) * g + b


def _rms_norm(y, g):
    return y * lax.rsqrt(jnp.mean(y * y, axis=-1, keepdims=True) + LN_EPS) * g


def _dot_nt(a, b):
    return lax.dot_general(a, b, (((1,), (1,)), ((), ())), preferred_element_type=F32)


def _layer_kernel(alpha,
                  x_ref, w_in_ref, b_in_ref, sinks_ref, tbl_ref, cw_ref, cb_ref, clg_ref, clb_ref,
                  ag_ref, cg_ref, w_out_ref, b_out_ref, l1g_ref, l1b_ref,
                  w_up_ref, fw_ref, fb_ref, wd_ref, l2g_ref, l2b_ref,
                  o_ref,
                  bias_ref, kv_ref, hbuf_ref, hshift_ref, hc_ref, ya_ref, y_ref, x1_ref, perm_ref, nat_ref,
                  cg_carry_ref, cu_carry_ref):
    ts = x_ref.shape[0]
    d_attn = N_HEADS * HEAD_DIM
    d_kv = N_KV_HEADS * HEAD_DIM
    d_conv = cw_ref.shape[1]
    d_ff = wd_ref.shape[0]
    n_chunks = d_ff // FFN_CHUNK
    b = pl.program_id(0)
    i = pl.program_id(1)

    qi = lax.broadcasted_iota(jnp.int32, (WINDOW, WINDOW), 0)
    kj = lax.broadcasted_iota(jnp.int32, (WINDOW, WINDOW), 1)
    tri = kj <= qi

    @pl.when((b == 0) & (i == 0))
    def _build_bias():
        n = (qi - kj) & (WINDOW - 1)
        max_exact = N_BUCKETS // 2
        nf = jnp.maximum(n, max_exact).astype(F32)
        large = max_exact + (jnp.log(nf / max_exact) / math.log(MAX_DISTANCE / max_exact)
                             * (N_BUCKETS - max_exact)).astype(jnp.int32)
        large = jnp.minimum(large, N_BUCKETS - 1)
        bucket = jnp.where(n < max_exact, n, large)
        for hd in range(N_HEADS):
            bias = jnp.zeros((WINDOW, WINDOW), F32)
            for bkt in range(N_BUCKETS):
                bias = jnp.where(bucket == bkt, tbl_ref[bkt * N_HEADS + hd], bias)
            bias_ref[hd] = bias

    @pl.when(i == 0)
    def _reset_history():
        kv_ref[:, 0:WINDOW, :] = jnp.zeros((8, WINDOW, d_kv), BF16)
        hbuf_ref[0:CONV_HALO, :] = jnp.zeros((CONV_HALO, d_conv), F32)
        cg_carry_ref[...] = jnp.zeros(cg_carry_ref.shape, F32)
        cu_carry_ref[...] = jnp.zeros(cu_carry_ref.shape, F32)

    x = x_ref[...]
    proj = jnp.dot(x.astype(BF16), w_in_ref[...], preferred_element_type=F32) + b_in_ref[...]
    q_end, k_end, v_end, a_end = d_attn, d_attn + d_kv, d_attn + 2 * d_kv, d_attn + 2 * d_kv + d_conv

    lane = lax.broadcasted_iota(jnp.int32, (ts, d_kv), 1)
    lo = lane < HEAD_DIM
    for base, t in ((0, proj[:, q_end:k_end]), (4, proj[:, k_end:v_end])):
        tr = pltpu.roll(t, HEAD_DIM, axis=1)
        zero = jnp.zeros_like(t)
        kv_ref[base + 0, WINDOW:WINDOW + ts, :] = jnp.where(lo, t, zero).astype(BF16)
        kv_ref[base + 1, WINDOW:WINDOW + ts, :] = jnp.where(lo, zero, tr).astype(BF16)
        kv_ref[base + 2, WINDOW:WINDOW + ts, :] = jnp.where(lo, tr, zero).astype(BF16)
        kv_ref[base + 3, WINDOW:WINDOW + ts, :] = jnp.where(lo, zero, t).astype(BF16)

    qs = (proj[:, 0:q_end] * (HEAD_DIM ** -0.5)).astype(BF16)
    lo_w = lax.broadcasted_iota(jnp.int32, (WINDOW, 2 * HEAD_DIM), 1) < HEAD_DIM
    for j in range(ts // WINDOW):
        r0 = j * WINDOW
        has_prev = (i > 0) | (j > 0)
        prev_mask = jnp.where(has_prev, 0.0, -jnp.inf).astype(F32)
        qb = qs[r0:r0 + WINDOW, :]
        pair_out = []
        for h in range(N_KV_HEADS):
            q2 = jnp.concatenate([qb[:, (2 * h) * 128:(2 * h + 1) * 128],
                                  qb[:, (2 * h + 1) * 128:(2 * h + 2) * 128]], axis=0)
            probs = {}
            inv = {}
            for ab in range(2):
                keys = kv_ref[2 * h + ab, r0:r0 + 2 * WINDOW, :]
                s = _dot_nt(q2, keys)
                for pp in range(2):
                    hd = 4 * h + 2 * pp + ab
                    s_prev = s[pp * WINDOW:(pp + 1) * WINDOW, 0:WINDOW]
                    s_cur = s[pp * WINDOW:(pp + 1) * WINDOW, WINDOW:2 * WINDOW]
                    sc = jnp.where(tri, s_cur, s_prev + prev_mask) + bias_ref[hd]
                    sink = sinks_ref[hd]
                    m = jnp.maximum(jnp.max(sc, axis=-1, keepdims=True), sink)
                    p = jnp.exp(sc - m)
                    denom = jnp.sum(p, axis=-1, keepdims=True) + jnp.exp(sink - m)
                    zero = jnp.zeros_like(p)
                    probs[(pp, ab)] = jnp.concatenate(
                        [jnp.where(tri, zero, p), jnp.where(tri, p, zero)], axis=1).astype(BF16)
                    inv[(pp, ab)] = 1.0 / denom
            for pp in range(2):
                va = kv_ref[4 + 2 * h + 0, r0:r0 + 2 * WINDOW, :]
                vb = kv_ref[4 + 2 * h + 1, r0:r0 + 2 * WINDOW, :]
                o = (jnp.dot(probs[(pp, 0)], va, preferred_element_type=F32)
                     + jnp.dot(probs[(pp, 1)], vb, preferred_element_type=F32))
                pair_out.append(o * jnp.where(lo_w, inv[(pp, 0)], inv[(pp, 1)]))
        ya_ref[r0:r0 + WINDOW, :] = jnp.concatenate(pair_out, axis=1)
    kv_ref[:, 0:WINDOW, :] = kv_ref[:, ts:ts + WINDOW, :]

    y_ref[:, 0:d_attn] = _rms_norm(ya_ref[...], ag_ref[...]).astype(BF16)

    hbuf_ref[CONV_HALO:CONV_HALO + ts, :] = proj[:, v_end:a_end] * jax.nn.sigmoid(proj[:, a_end:])
    for s in range(1, 8):
        hshift_ref[s - 1] = hbuf_ref[s:s + hshift_ref.shape[1], :]

    def conv_chunk(c, carry):
        r0 = pl.multiple_of(c * CONV_ROWS, CONV_ROWS)
        acc = jnp.broadcast_to(cb_ref[...], (CONV_ROWS, d_conv))
        for k in range(CONV_WIDTH):
            off = CONV_HALO - (CONV_WIDTH - 1) + k
            s, aligned = off % 8, off - off % 8
            rows = pl.ds(r0 + aligned, CONV_ROWS)
            tap = hbuf_ref[rows, :] if s == 0 else hshift_ref[s - 1, rows, :]
            acc = acc + cw_ref[k:k + 1, :] * tap
        hc_ref[pl.ds(r0, CONV_ROWS), :] = acc
        return carry

    lax.fori_loop(0, ts // CONV_ROWS, conv_chunk, 0)
    hbuf_ref[0:CONV_HALO, :] = hbuf_ref[ts:ts + CONV_HALO, :]
    hn = _layer_norm(hc_ref[...], clg_ref[...], clb_ref[...])
    y_ref[:, d_attn:d_attn + d_conv] = _rms_norm(hn * jax.nn.sigmoid(hn), cg_ref[...]).astype(BF16)

    mix = jnp.dot(y_ref[...], w_out_ref[...], preferred_element_type=F32) + b_out_ref[...]
    x1 = _layer_norm(alpha * x + mix, l1g_ref[...], l1b_ref[...])
    x1_ref[...] = x1

    seg = ts // 8
    n_slabs = x1_ref.shape[1] // LANES
    for l in range(n_slabs):
        for s in range(8):
            for u in range(seg // 8):
                n0 = s * seg + 8 * u
                perm_ref[l, pl.ds(seg * u + s, 8, stride=8), :] = x1[n0:n0 + 8, l * LANES:(l + 1) * LANES]
    x1p = jnp.concatenate([perm_ref[l] for l in range(n_slabs)], axis=1).astype(BF16)

    first_sublane = lax.broadcasted_iota(jnp.int32, (8, FFN_CHUNK), 0) == 0

    def causal_taps(up, carry_ref, c):
        prev = carry_ref[c]
        carry_ref[c] = up[ts - 16:ts]
        roll1 = lambda t: pltpu.roll(t, 1, axis=0)
        head1 = jnp.where(first_sublane, roll1(prev[8:16]), roll1(up[ts - 8:ts]))
        head2 = jnp.where(first_sublane, roll1(prev[0:8]), roll1(up[ts - 16:ts - 8]))
        return (jnp.concatenate([head1, up[0:ts - 8]], axis=0),
                jnp.concatenate([head2, head1, up[0:ts - 16]], axis=0))

    def conv3(col0, carry_ref, c):
        cols = slice(col0 + c * FFN_CHUNK, col0 + (c + 1) * FFN_CHUNK)
        up = jnp.dot(x1p, w_up_ref[:, cols], preferred_element_type=F32)
        up1, up2 = causal_taps(up, carry_ref, c)
        return (fw_ref[2:3, cols] * up + (fw_ref[1:2, cols] * up1 + (fw_ref[0:1, cols] * up2 + fb_ref[:, cols])))

    ffn = None
    for c in range(n_chunks):
        hg = conv3(0, cg_carry_ref, c)
        hu = conv3(d_ff, cu_carry_ref, c)
        act = (hg * jax.nn.sigmoid(hg) * hu).astype(BF16)
        part = jnp.dot(act, wd_ref[c * FFN_CHUNK:(c + 1) * FFN_CHUNK, :], preferred_element_type=F32)
        ffn = part if ffn is None else ffn + part

    pitch = nat_ref.shape[1] // 8
    for l in range(n_slabs):
        for v in range(seg):
            nat_ref[l, pl.ds(v, 8, stride=pitch), :] = ffn[v * 8:(v + 1) * 8, l * LANES:(l + 1) * LANES]
    ffn_t = jnp.concatenate(
        [jnp.concatenate([nat_ref[l, s * pitch:s * pitch + seg, :] for s in range(8)], axis=0)
         for l in range(n_slabs)], axis=1)

    o_ref[...] = _layer_norm(alpha * x1_ref[...] + ffn_t, l2g_ref[...], l2b_ref[...])


def _full_spec(shape, single_buffer=False):
    index_map = lambda b, i: (0,) * len(shape)
    if single_buffer:
        return pl.BlockSpec(shape, index_map, pipeline_mode=pl.Buffered(1))
    return pl.BlockSpec(shape, index_map)


def _layer(x, alpha, w_in, b_in, sinks, tbl, cw, cb, clg, clb, ag, cg, w_out, b_out, l1g, l1b,
           w_up, fw, fb, w_down, l2g, l2b, seq_tile):
    bsz, seq, d_model = x.shape
    d_conv = cw.shape[1]
    d_ff = w_down.shape[0]
    d_kv = N_KV_HEADS * HEAD_DIM
    n_chunks = d_ff // FFN_CHUNK
    seg = seq_tile // 8
    nat_pitch = seg + 8
    assert seq % seq_tile == 0 and seq_tile % WINDOW == 0 and d_ff % FFN_CHUNK == 0
    assert seg % 16 == 0 and d_model % LANES == 0

    row = lambda v: v.reshape(1, -1).astype(F32)
    args = [
        x,
        w_in.astype(BF16), row(b_in), sinks.astype(F32), tbl.reshape(-1).astype(F32),
        cw.astype(F32), row(cb), row(clg), row(clb), row(ag), row(cg),
        w_out.astype(BF16), row(b_out), row(l1g), row(l1b),
        w_up.astype(BF16), fw.astype(F32), row(fb), w_down.astype(BF16), row(l2g), row(l2b),
    ]
    smem = pl.BlockSpec(memory_space=pltpu.SMEM)
    big = {1, 11, 15, 18}
    in_specs = [pl.BlockSpec((None, seq_tile, d_model), lambda b, i: (b, i, 0))]
    for idx, a in enumerate(args[1:], start=1):
        in_specs.append(smem if idx in (3, 4) else _full_spec(a.shape, single_buffer=idx in big))

    scratch = [
        pltpu.VMEM((N_HEADS, WINDOW, WINDOW), F32),
        pltpu.VMEM((8, seq_tile + WINDOW, d_kv), BF16),
        pltpu.VMEM((seq_tile + CONV_HALO, d_conv), F32),
        pltpu.VMEM((7, seq_tile + CONV_HALO - 8, d_conv), F32),
        pltpu.VMEM((seq_tile, d_conv), F32),
        pltpu.VMEM((seq_tile, N_HEADS * HEAD_DIM), F32),
        pltpu.VMEM((seq_tile, d_model), BF16),
        pltpu.VMEM((seq_tile, d_model), F32),
        pltpu.VMEM((d_model // LANES, seq_tile, LANES), F32),
        pltpu.VMEM((d_model // LANES, 8 * nat_pitch, LANES), F32),
        pltpu.VMEM((n_chunks, 16, FFN_CHUNK), F32),
        pltpu.VMEM((n_chunks, 16, FFN_CHUNK), F32),
    ]
    return pl.pallas_call(
        functools.partial(_layer_kernel, alpha),
        grid=(bsz, seq // seq_tile),
        in_specs=in_specs,
        out_specs=pl.BlockSpec((None, seq_tile, d_model), lambda b, i: (b, i, 0)),
        out_shape=jax.ShapeDtypeStruct(x.shape, x.dtype),
        scratch_shapes=scratch,
        compiler_params=pltpu.CompilerParams(
            dimension_semantics=("arbitrary", "arbitrary"),
            vmem_limit_bytes=VMEM_LIMIT_BYTES),
        name="hybrid_layer",
    )(*args)


def kernel(x, w_in, b_in, attn_sinks, rel_bias_table, conv_dw_w, conv_dw_b, conv_ln_g, conv_ln_b,
           attn_out_gain, conv_out_gain, w_out, b_out, ln1_g, ln1_b, w_up, ffn_dw_w, ffn_dw_b, w_down,
           ln2_g, ln2_b, seq_tile=SEQ_TILE):
    depth = w_in.shape[0]
    alpha = (2.0 * depth) ** 0.25
    for l in range(depth):
        x = _layer(x, alpha, w_in[l], b_in[l], attn_sinks[l], rel_bias_table, conv_dw_w[l], conv_dw_b[l],
                   conv_ln_g[l], conv_ln_b[l], attn_out_gain[l], conv_out_gain[l], w_out[l], b_out[l],
                   ln1_g[l], ln1_b[l], w_up[l], ffn_dw_w[l], ffn_dw_b[l], w_down[l], ln2_g[l], ln2_b[l],
                   seq_tile)
    return x
```

```python
import functools
import math

import jax
import jax.numpy as jnp
from jax import lax
from jax.experimental import pallas as pl
from jax.experimental.pallas import tpu as pltpu

HEAD_DIM = 64
N_KV_HEADS = 2
GQA_GROUP = 4
N_HEADS = N_KV_HEADS * GQA_GROUP
WINDOW = 128
CONV_WIDTH = 31
N_BUCKETS = 32
MAX_DISTANCE = 128
FFN_CONV_WIDTH = 3
LN_EPS = 1e-5

SEQ_TILE = 512
CONV_HALO = 32
CONV_ROWS = 32
FFN_CHUNK = 256
LANES = 128
VMEM_LIMIT_BYTES = 58 * 1024 * 1024

BF16 = jnp.bfloat16
F32 = jnp.float32


def _layer_norm(r, g, b):
    mu = jnp.mean(r, axis=-1, keepdims=True)
    d = r - mu
    var = jnp.mean(d * d, axis=-1, keepdims=True)
    return d * lax.rsqrt(var + LN_EPS) * g + b


def _rms_norm(y, g):
    return y * lax.rsqrt(jnp.mean(y * y, axis=-1, keepdims=True) + LN_EPS) * g


def _dot_nt(a, b):
    return lax.dot_general(a, b, (((1,), (1,)), ((), ())), preferred_element_type=F32)


def _layer_kernel(alpha, n_tiles, tiles_per_row,
                  x_ref, w_in_ref, b_in_ref, sinks_ref, tbl_ref, cw_ref, cb_ref, clg_ref, clb_ref,
                  ag_ref, cg_ref, w_out_ref, b_out_ref, l1g_ref, l1b_ref,
                  w_up_ref, fw_ref, fb_ref, wd_ref, l2g_ref, l2b_ref,
                  o_ref,
                  bias_ref, kv_ref, hbuf_ref, hshift_ref, ya_ref, y_ref, x1_ref, perm_ref, nat_ref,
                  cg_carry_ref, cu_carry_ref):
    ts = x_ref.shape[0]
    d_model = x_ref.shape[1]
    d_attn = N_HEADS * HEAD_DIM
    d_kv = N_KV_HEADS * HEAD_DIM
    d_conv = cw_ref.shape[1]
    d_ff = wd_ref.shape[0]
    n_chunks = d_ff // FFN_CHUNK
    seg = ts // 8
    n_slabs = d_model // LANES
    g = pl.program_id(0)
    i = jnp.minimum(g, n_tiles - 1) % tiles_per_row
    fi = jnp.maximum(g - 1, 0) % tiles_per_row

    qi = lax.broadcasted_iota(jnp.int32, (WINDOW, WINDOW), 0)
    kj = lax.broadcasted_iota(jnp.int32, (WINDOW, WINDOW), 1)
    tri = kj <= qi

    @pl.when(g == 0)
    def _first_step():
        x1_ref[...] = jnp.zeros(x1_ref.shape, F32)
        n = (qi - kj) & (WINDOW - 1)
        max_exact = N_BUCKETS // 2
        nf = jnp.maximum(n, max_exact).astype(F32)
        large = max_exact + (jnp.log(nf / max_exact) / math.log(MAX_DISTANCE / max_exact)
                             * (N_BUCKETS - max_exact)).astype(jnp.int32)
        large = jnp.minimum(large, N_BUCKETS - 1)
        bucket = jnp.where(n < max_exact, n, large)
        for hd in range(N_HEADS):
            bias = jnp.zeros((WINDOW, WINDOW), F32)
            for bkt in range(N_BUCKETS):
                bias = jnp.where(bucket == bkt, tbl_ref[bkt * N_HEADS + hd], bias)
            bias_ref[hd] = bias

    @pl.when(i == 0)
    def _reset_mixer_history():
        kv_ref[:, 0:WINDOW, :] = jnp.zeros((8, WINDOW, d_kv), BF16)
        hbuf_ref[0:CONV_HALO, :] = jnp.zeros((CONV_HALO, d_conv), F32)

    @pl.when(fi == 0)
    def _reset_ffn_history():
        cg_carry_ref[...] = jnp.zeros(cg_carry_ref.shape, F32)
        cu_carry_ref[...] = jnp.zeros(cu_carry_ref.shape, F32)

    def mixer():
        x = x_ref[...]
        proj = jnp.dot(x.astype(BF16), w_in_ref[...], preferred_element_type=F32) + b_in_ref[...]
        q_end, k_end, v_end, a_end = d_attn, d_attn + d_kv, d_attn + 2 * d_kv, d_attn + 2 * d_kv + d_conv
        yield 0

        lane = lax.broadcasted_iota(jnp.int32, (ts, d_kv), 1)
        lo = lane < HEAD_DIM
        for base, t in ((0, proj[:, q_end:k_end]), (4, proj[:, k_end:v_end])):
            tr = pltpu.roll(t, HEAD_DIM, axis=1)
            zero = jnp.zeros_like(t)
            kv_ref[base + 0, WINDOW:WINDOW + ts, :] = jnp.where(lo, t, zero).astype(BF16)
            kv_ref[base + 1, WINDOW:WINDOW + ts, :] = jnp.where(lo, zero, tr).astype(BF16)
            kv_ref[base + 2, WINDOW:WINDOW + ts, :] = jnp.where(lo, tr, zero).astype(BF16)
            kv_ref[base + 3, WINDOW:WINDOW + ts, :] = jnp.where(lo, zero, t).astype(BF16)
            yield 500

        hbuf_ref[CONV_HALO:CONV_HALO + ts, :] = proj[:, v_end:a_end] * jax.nn.sigmoid(proj[:, a_end:])
        yield 1300
        for s in range(1, 8):
            hshift_ref[s - 1] = hbuf_ref[s:s + hshift_ref.shape[1], :]
            yield 850

        qs = (proj[:, 0:q_end] * (HEAD_DIM ** -0.5)).astype(BF16)
        yield 400
        lo_w = lax.broadcasted_iota(jnp.int32, (WINDOW, 2 * HEAD_DIM), 1) < HEAD_DIM
        for j in range(ts // WINDOW):
            r0 = j * WINDOW
            has_prev = (i > 0) | (j > 0)
            prev_mask = jnp.where(has_prev, 0.0, -jnp.inf).astype(F32)
            qb = qs[r0:r0 + WINDOW, :]
            pair_out = []
            for h in range(N_KV_HEADS):
                q2 = jnp.concatenate([qb[:, (2 * h) * 128:(2 * h + 1) * 128],
                                      qb[:, (2 * h + 1) * 128:(2 * h + 2) * 128]], axis=0)
                probs = {}
                inv = {}
                for ab in range(2):
                    keys = kv_ref[2 * h + ab, r0:r0 + 2 * WINDOW, :]
                    s = _dot_nt(q2, keys)
                    for pp in range(2):
                        hd = 4 * h + 2 * pp + ab
                        s_prev = s[pp * WINDOW:(pp + 1) * WINDOW, 0:WINDOW]
                        s_cur = s[pp * WINDOW:(pp + 1) * WINDOW, WINDOW:2 * WINDOW]
                        sc = jnp.where(tri, s_cur, s_prev + prev_mask) + bias_ref[hd]
                        sink = sinks_ref[hd]
                        m = jnp.maximum(jnp.max(sc, axis=-1, keepdims=True), sink)
                        p = jnp.exp(sc - m)
                        denom = jnp.sum(p, axis=-1, keepdims=True) + jnp.exp(sink - m)
                        zero = jnp.zeros_like(p)
                        probs[(pp, ab)] = jnp.concatenate(
                            [jnp.where(tri, zero, p), jnp.where(tri, p, zero)], axis=1).astype(BF16)
                        inv[(pp, ab)] = 1.0 / denom
                for pp in range(2):
                    va = kv_ref[4 + 2 * h + 0, r0:r0 + 2 * WINDOW, :]
                    vb = kv_ref[4 + 2 * h + 1, r0:r0 + 2 * WINDOW, :]
                    o = (jnp.dot(probs[(pp, 0)], va, preferred_element_type=F32)
                         + jnp.dot(probs[(pp, 1)], vb, preferred_element_type=F32))
                    pair_out.append(o * jnp.where(lo_w, inv[(pp, 0)], inv[(pp, 1)]))
                yield 1300
            ya_ref[r0:r0 + WINDOW, :] = jnp.concatenate(pair_out, axis=1)
        kv_ref[:, 0:WINDOW, :] = kv_ref[:, ts:ts + WINDOW, :]
        y_ref[:, 0:d_attn] = _rms_norm(ya_ref[...], ag_ref[...]).astype(BF16)
        yield 1000

        for c in range(ts // CONV_ROWS):
            r0 = c * CONV_ROWS
            acc = jnp.broadcast_to(cb_ref[...], (CONV_ROWS, d_conv))
            for k in range(CONV_WIDTH):
                off = CONV_HALO - (CONV_WIDTH - 1) + k
                s, aligned = off % 8, off - off % 8
                rows = slice(r0 + aligned, r0 + aligned + CONV_ROWS)
                tap = hbuf_ref[rows, :] if s == 0 else hshift_ref[s - 1, rows, :]
                acc = acc + cw_ref[k:k + 1, :] * tap
            hn = _layer_norm(acc, clg_ref[...], clb_ref[...])
            y_ref[r0:r0 + CONV_ROWS, d_attn:d_attn + d_conv] = (
                _rms_norm(hn * jax.nn.sigmoid(hn), cg_ref[...]).astype(BF16))
            yield 1550
        hbuf_ref[0:CONV_HALO, :] = hbuf_ref[ts:ts + CONV_HALO, :]

        mix = jnp.dot(y_ref[...], w_out_ref[...], preferred_element_type=F32) + b_out_ref[...]
        x1_ref[...] = _layer_norm(alpha * x + mix, l1g_ref[...], l1b_ref[...])
        yield 0

    mixer_pieces = mixer()
    interleavable = 2 * 500 + 1300 + 7 * 850 + 400 + 8 * 1300 + 1000 + (ts // CONV_ROWS) * 1550

    emitted = [0]

    def emit_mixer(target):
        while emitted[0] < target:
            emitted[0] += next(mixer_pieces)

    x1_prev = x1_ref[...]
    for l in range(n_slabs):
        for s in range(8):
            for u in range(seg // 8):
                n0 = s * seg + 8 * u
                perm_ref[l, pl.ds(seg * u + s, 8, stride=8), :] = x1_prev[n0:n0 + 8, l * LANES:(l + 1) * LANES]
    x1pb = jnp.concatenate([perm_ref[l] for l in range(n_slabs)], axis=1).astype(BF16)
    next(mixer_pieces)

    first_sublane = lax.broadcasted_iota(jnp.int32, (8, FFN_CHUNK), 0) == 0

    def causal_taps(up, carry_ref, c):
        prev = carry_ref[c]
        carry_ref[c] = up[ts - 16:ts]
        roll1 = lambda t: pltpu.roll(t, 1, axis=0)
        head1 = jnp.where(first_sublane, roll1(prev[8:16]), roll1(up[ts - 8:ts]))
        head2 = jnp.where(first_sublane, roll1(prev[0:8]), roll1(up[ts - 16:ts - 8]))
        return (jnp.concatenate([head1, up[0:ts - 8]], axis=0),
                jnp.concatenate([head2, head1, up[0:ts - 16]], axis=0))

    def conv3(col0, carry_ref, c):
        cols = slice(col0 + c * FFN_CHUNK, col0 + (c + 1) * FFN_CHUNK)
        up = jnp.dot(x1pb, w_up_ref[:, cols], preferred_element_type=F32)
        up1, up2 = causal_taps(up, carry_ref, c)
        return (fw_ref[2:3, cols] * up + (fw_ref[1:2, cols] * up1 + (fw_ref[0:1, cols] * up2 + fb_ref[:, cols])))

    ffn = None
    for c in range(n_chunks):
        hg = conv3(0, cg_carry_ref, c)
        hu = conv3(d_ff, cu_carry_ref, c)
        act = (hg * jax.nn.sigmoid(hg) * hu).astype(BF16)
        part = jnp.dot(act, wd_ref[c * FFN_CHUNK:(c + 1) * FFN_CHUNK, :], preferred_element_type=F32)
        ffn = part if ffn is None else ffn + part
        emit_mixer(interleavable * (c + 1) // n_chunks)
    for _ in mixer_pieces:
        pass

    x1p = jnp.concatenate([perm_ref[l] for l in range(n_slabs)], axis=1)
    out_p = _layer_norm(alpha * x1p + ffn, l2g_ref[...], l2b_ref[...])

    pitch = nat_ref.shape[1] // 8
    for l in range(n_slabs):
        for v in range(seg):
            nat_ref[l, pl.ds(v, 8, stride=pitch), :] = out_p[v * 8:(v + 1) * 8, l * LANES:(l + 1) * LANES]
    for l in range(n_slabs):
        for s in range(8):
            o_ref[s * seg:(s + 1) * seg, l * LANES:(l + 1) * LANES] = nat_ref[l, s * pitch:s * pitch + seg, :]


def _full_spec(shape, single_buffer=False):
    index_map = lambda g: (0,) * len(shape)
    if single_buffer:
        return pl.BlockSpec(shape, index_map, pipeline_mode=pl.Buffered(1))
    return pl.BlockSpec(shape, index_map)


def _layer(x, alpha, w_in, b_in, sinks, tbl, cw, cb, clg, clb, ag, cg, w_out, b_out, l1g, l1b,
           w_up, fw, fb, w_down, l2g, l2b, seq_tile):
    bsz, seq, d_model = x.shape
    d_conv = cw.shape[1]
    d_ff = w_down.shape[0]
    d_kv = N_KV_HEADS * HEAD_DIM
    n_chunks = d_ff // FFN_CHUNK
    tiles_per_row = seq // seq_tile
    n_tiles = bsz * tiles_per_row
    seg = seq_tile // 8
    nat_pitch = seg + 8
    assert seq % seq_tile == 0 and seq_tile % WINDOW == 0 and d_ff % FFN_CHUNK == 0
    assert seg % 16 == 0 and d_model % LANES == 0

    row = lambda v: v.reshape(1, -1).astype(F32)
    args = [
        x,
        w_in.astype(BF16), row(b_in), sinks.astype(F32), tbl.reshape(-1).astype(F32),
        cw.astype(F32), row(cb), row(clg), row(clb), row(ag), row(cg),
        w_out.astype(BF16), row(b_out), row(l1g), row(l1b),
        w_up.astype(BF16), fw.astype(F32), row(fb), w_down.astype(BF16), row(l2g), row(l2b),
    ]
    smem = pl.BlockSpec(memory_space=pltpu.SMEM)
    big = {1, 11, 15, 18}

    def mixer_tile(g):
        t = jnp.minimum(g, n_tiles - 1)
        return (t // tiles_per_row, t % tiles_per_row, 0)

    def ffn_tile(g):
        t = jnp.maximum(g - 1, 0)
        return (t // tiles_per_row, t % tiles_per_row, 0)

    in_specs = [pl.BlockSpec((None, seq_tile, d_model), mixer_tile)]
    for idx, a in enumerate(args[1:], start=1):
        in_specs.append(smem if idx in (3, 4) else _full_spec(a.shape, single_buffer=idx in big))

    scratch = [
        pltpu.VMEM((N_HEADS, WINDOW, WINDOW), F32),
        pltpu.VMEM((8, seq_tile + WINDOW, d_kv), BF16),
        pltpu.VMEM((seq_tile + CONV_HALO, d_conv), F32),
        pltpu.VMEM((7, seq_tile + CONV_HALO - 8, d_conv), F32),
        pltpu.VMEM((seq_tile, N_HEADS * HEAD_DIM), F32),
        pltpu.VMEM((seq_tile, d_model), BF16),
        pltpu.VMEM((seq_tile, d_model), F32),
        pltpu.VMEM((d_model // LANES, seq_tile, LANES), F32),
        pltpu.VMEM((d_model // LANES, 8 * nat_pitch, LANES), F32),
        pltpu.VMEM((n_chunks, 16, FFN_CHUNK), F32),
        pltpu.VMEM((n_chunks, 16, FFN_CHUNK), F32),
    ]
    return pl.pallas_call(
        functools.partial(_layer_kernel, alpha, n_tiles, tiles_per_row),
        grid=(n_tiles + 1,),
        in_specs=in_specs,
        out_specs=pl.BlockSpec((None, seq_tile, d_model), ffn_tile),
        out_shape=jax.ShapeDtypeStruct(x.shape, x.dtype),
        scratch_shapes=scratch,
        compiler_params=pltpu.CompilerParams(
            dimension_semantics=("arbitrary",),
            vmem_limit_bytes=VMEM_LIMIT_BYTES),
        name="hybrid_layer",
    )(*args)


def kernel(x, w_in, b_in, attn_sinks, rel_bias_table, conv_dw_w, conv_dw_b, conv_ln_g, conv_ln_b,
           attn_out_gain, conv_out_gain, w_out, b_out, ln1_g, ln1_b, w_up, ffn_dw_w, ffn_dw_b, w_down,
           ln2_g, ln2_b, seq_tile=SEQ_TILE):
    depth = w_in.shape[0]
    alpha = (2.0 * depth) ** 0.25
    for l in range(depth):
        x = _layer(x, alpha, w_in[l], b_in[l], attn_sinks[l], rel_bias_table, conv_dw_w[l], conv_dw_b[l],
                   conv_ln_g[l], conv_ln_b[l], attn_out_gain[l], conv_out_gain[l], w_out[l], b_out[l],
                   ln1_g[l], ln1_b[l], w_up[l], ffn_dw_w[l], ffn_dw_b[l], w_down[l], ln2_g[l], ln2_b[l],
                   seq_tile)
    return x
```

```python
import functools
import math

import jax
import jax.numpy as jnp
from jax import lax
from jax.experimental import pallas as pl
from jax.experimental.pallas import tpu as pltpu

HEAD_DIM = 64
N_KV_HEADS = 2
GQA_GROUP = 4
N_HEADS = N_KV_HEADS * GQA_GROUP
WINDOW = 128
CONV_WIDTH = 31
N_BUCKETS = 32
MAX_DISTANCE = 128
FFN_CONV_WIDTH = 3
LN_EPS = 1e-5

SEQ_TILE = 512
CONV_HALO = 32
CONV_ROWS = 32
FFN_CHUNK = 256
LANES = 128
VMEM_LIMIT_BYTES = 58 * 1024 * 1024

BF16 = jnp.bfloat16
F32 = jnp.float32


def _layer_norm(r, g, b):
    mu = jnp.mean(r, axis=-1, keepdims=True)
    d = r - mu
    var = jnp.mean(d * d, axis=-1, keepdims=True)
    return d * lax.rsqrt(var + LN_EPS) * g + b


def _rms_norm(y, g):
    return y * lax.rsqrt(jnp.mean(y * y, axis=-1, keepdims=True) + LN_EPS) * g


def _dot_nt(a, b):
    return lax.dot_general(a, b, (((1,), (1,)), ((), ())), preferred_element_type=F32)


def _layer_kernel(alpha, n_tiles, tiles_per_row,
                  x_ref, w_in_ref, b_in_ref, sinks_ref, tbl_ref, cw_ref, cb_ref, clg_ref, clb_ref,
                  ag_ref, cg_ref, w_out_ref, b_out_ref, l1g_ref, l1b_ref,
                  w_up_ref, fw_ref, fb_ref, wd_ref, l2g_ref, l2b_ref,
                  o_ref,
                  bias_ref, kv_ref, hbuf_ref, hshift_ref, ya_ref, y_ref, x1_ref, perm_ref, nat_ref,
                  act_ref, cg_carry_ref, cu_carry_ref):
    ts = x_ref.shape[0]
    d_model = x_ref.shape[1]
    d_attn = N_HEADS * HEAD_DIM
    d_kv = N_KV_HEADS * HEAD_DIM
    d_conv = cw_ref.shape[1]
    d_ff = wd_ref.shape[0]
    n_chunks = d_ff // FFN_CHUNK
    seg = ts // 8
    n_slabs = d_model // LANES
    g = pl.program_id(0)
    i = jnp.minimum(g, n_tiles - 1) % tiles_per_row
    fi = jnp.maximum(g - 1, 0) % tiles_per_row

    qi = lax.broadcasted_iota(jnp.int32, (WINDOW, WINDOW), 0)
    kj = lax.broadcasted_iota(jnp.int32, (WINDOW, WINDOW), 1)
    tri = kj <= qi

    @pl.when(g == 0)
    def _first_step():
        x1_ref[...] = jnp.zeros(x1_ref.shape, F32)
        n = (qi - kj) & (WINDOW - 1)
        max_exact = N_BUCKETS // 2
        nf = jnp.maximum(n, max_exact).astype(F32)
        large = max_exact + (jnp.log(nf / max_exact) / math.log(MAX_DISTANCE / max_exact)
                             * (N_BUCKETS - max_exact)).astype(jnp.int32)
        large = jnp.minimum(large, N_BUCKETS - 1)
        bucket = jnp.where(n < max_exact, n, large)
        for hd in range(N_HEADS):
            bias = jnp.zeros((WINDOW, WINDOW), F32)
            for bkt in range(N_BUCKETS):
                bias = jnp.where(bucket == bkt, tbl_ref[bkt * N_HEADS + hd], bias)
            bias_ref[hd] = bias

    @pl.when(i == 0)
    def _reset_mixer_history():
        kv_ref[:, 0:WINDOW, :] = jnp.zeros((8, WINDOW, d_kv), BF16)
        hbuf_ref[0:CONV_HALO, :] = jnp.zeros((CONV_HALO, d_conv), F32)

    @pl.when(fi == 0)
    def _reset_ffn_history():
        cg_carry_ref[...] = jnp.zeros(cg_carry_ref.shape, F32)
        cu_carry_ref[...] = jnp.zeros(cu_carry_ref.shape, F32)

    def mixer():
        x = x_ref[...]
        proj = jnp.dot(x.astype(BF16), w_in_ref[...], preferred_element_type=F32) + b_in_ref[...]
        q_end, k_end, v_end, a_end = d_attn, d_attn + d_kv, d_attn + 2 * d_kv, d_attn + 2 * d_kv + d_conv
        yield 0

        lane = lax.broadcasted_iota(jnp.int32, (ts, d_kv), 1)
        lo = lane < HEAD_DIM
        for base, t in ((0, proj[:, q_end:k_end]), (4, proj[:, k_end:v_end])):
            tr = pltpu.roll(t, HEAD_DIM, axis=1)
            zero = jnp.zeros_like(t)
            kv_ref[base + 0, WINDOW:WINDOW + ts, :] = jnp.where(lo, t, zero).astype(BF16)
            kv_ref[base + 1, WINDOW:WINDOW + ts, :] = jnp.where(lo, zero, tr).astype(BF16)
            kv_ref[base + 2, WINDOW:WINDOW + ts, :] = jnp.where(lo, tr, zero).astype(BF16)
            kv_ref[base + 3, WINDOW:WINDOW + ts, :] = jnp.where(lo, zero, t).astype(BF16)
            yield 500

        hbuf_ref[CONV_HALO:CONV_HALO + ts, :] = proj[:, v_end:a_end] * jax.nn.sigmoid(proj[:, a_end:])
        yield 1300
        for s in range(1, 8):
            hshift_ref[s - 1] = hbuf_ref[s:s + hshift_ref.shape[1], :]
            yield 850

        qs = (proj[:, 0:q_end] * (HEAD_DIM ** -0.5)).astype(BF16)
        yield 400
        lo_w = lax.broadcasted_iota(jnp.int32, (WINDOW, 2 * HEAD_DIM), 1) < HEAD_DIM
        for j in range(ts // WINDOW):
            r0 = j * WINDOW
            has_prev = (i > 0) | (j > 0)
            prev_mask = jnp.where(has_prev, 0.0, -jnp.inf).astype(F32)
            qb = qs[r0:r0 + WINDOW, :]
            pair_out = []
            for h in range(N_KV_HEADS):
                q2 = jnp.concatenate([qb[:, (2 * h) * 128:(2 * h + 1) * 128],
                                      qb[:, (2 * h + 1) * 128:(2 * h + 2) * 128]], axis=0)
                probs = {}
                inv = {}
                for ab in range(2):
                    keys = kv_ref[2 * h + ab, r0:r0 + 2 * WINDOW, :]
                    s = _dot_nt(q2, keys)
                    for pp in range(2):
                        hd = 4 * h + 2 * pp + ab
                        s_prev = s[pp * WINDOW:(pp + 1) * WINDOW, 0:WINDOW]
                        s_cur = s[pp * WINDOW:(pp + 1) * WINDOW, WINDOW:2 * WINDOW]
                        sc = jnp.where(tri, s_cur, s_prev + prev_mask) + bias_ref[hd]
                        sink = sinks_ref[hd]
                        m = jnp.maximum(jnp.max(sc, axis=-1, keepdims=True), sink)
                        p = jnp.exp(sc - m)
                        denom = jnp.sum(p, axis=-1, keepdims=True) + jnp.exp(sink - m)
                        zero = jnp.zeros_like(p)
                        probs[(pp, ab)] = jnp.concatenate(
                            [jnp.where(tri, zero, p), jnp.where(tri, p, zero)], axis=1).astype(BF16)
                        inv[(pp, ab)] = 1.0 / denom
                for pp in range(2):
                    va = kv_ref[4 + 2 * h + 0, r0:r0 + 2 * WINDOW, :]
                    vb = kv_ref[4 + 2 * h + 1, r0:r0 + 2 * WINDOW, :]
                    o = (jnp.dot(probs[(pp, 0)], va, preferred_element_type=F32)
                         + jnp.dot(probs[(pp, 1)], vb, preferred_element_type=F32))
                    pair_out.append(o * jnp.where(lo_w, inv[(pp, 0)], inv[(pp, 1)]))
                yield 1300
            ya_ref[r0:r0 + WINDOW, :] = jnp.concatenate(pair_out, axis=1)
        kv_ref[:, 0:WINDOW, :] = kv_ref[:, ts:ts + WINDOW, :]
        y_ref[:, 0:d_attn] = _rms_norm(ya_ref[...], ag_ref[...]).astype(BF16)
        yield 1000

        for c in range(ts // CONV_ROWS):
            r0 = c * CONV_ROWS
            acc = jnp.broadcast_to(cb_ref[...], (CONV_ROWS, d_conv))
            for k in range(CONV_WIDTH):
                off = CONV_HALO - (CONV_WIDTH - 1) + k
                s, aligned = off % 8, off - off % 8
                rows = slice(r0 + aligned, r0 + aligned + CONV_ROWS)
                tap = hbuf_ref[rows, :] if s == 0 else hshift_ref[s - 1, rows, :]
                acc = acc + cw_ref[k:k + 1, :] * tap
            hn = _layer_norm(acc, clg_ref[...], clb_ref[...])
            y_ref[r0:r0 + CONV_ROWS, d_attn:d_attn + d_conv] = (
                _rms_norm(hn * jax.nn.sigmoid(hn), cg_ref[...]).astype(BF16))
            yield 1550
        hbuf_ref[0:CONV_HALO, :] = hbuf_ref[ts:ts + CONV_HALO, :]

        mix = jnp.dot(y_ref[...], w_out_ref[...], preferred_element_type=F32) + b_out_ref[...]
        x1_ref[...] = _layer_norm(alpha * x + mix, l1g_ref[...], l1b_ref[...])
        yield 0

    mixer_pieces = mixer()
    interleavable = 2 * 500 + 1300 + 7 * 850 + 400 + 8 * 1300 + 1000 + (ts // CONV_ROWS) * 1550

    emitted = [0]

    def emit_mixer(target):
        while emitted[0] < target:
            emitted[0] += next(mixer_pieces)

    x1_prev = x1_ref[...]
    for l in range(n_slabs):
        for s in range(8):
            for u in range(seg // 8):
                n0 = s * seg + 8 * u
                perm_ref[l, pl.ds(seg * u + s, 8, stride=8), :] = x1_prev[n0:n0 + 8, l * LANES:(l + 1) * LANES]
    x1pb = jnp.concatenate([perm_ref[l] for l in range(n_slabs)], axis=1).astype(BF16)
    next(mixer_pieces)

    first_sublane = lax.broadcasted_iota(jnp.int32, (8, FFN_CHUNK), 0) == 0

    def causal_taps(up, carry_ref, c):
        prev = carry_ref[c]
        carry_ref[c] = up[ts - 16:ts]
        roll1 = lambda t: pltpu.roll(t, 1, axis=0)
        head1 = jnp.where(first_sublane, roll1(prev[8:16]), roll1(up[ts - 8:ts]))
        head2 = jnp.where(first_sublane, roll1(prev[0:8]), roll1(up[ts - 16:ts - 8]))
        return (jnp.concatenate([head1, up[0:ts - 8]], axis=0),
                jnp.concatenate([head2, head1, up[0:ts - 16]], axis=0))

    def conv3(col0, carry_ref, c):
        cols = slice(col0 + c * FFN_CHUNK, col0 + (c + 1) * FFN_CHUNK)
        up = jnp.dot(x1pb, w_up_ref[:, cols], preferred_element_type=F32)
        up1, up2 = causal_taps(up, carry_ref, c)
        return (fw_ref[2:3, cols] * up + (fw_ref[1:2, cols] * up1 + (fw_ref[0:1, cols] * up2 + fb_ref[:, cols])))

    for c in range(n_chunks):
        hg = conv3(0, cg_carry_ref, c)
        hu = conv3(d_ff, cu_carry_ref, c)
        act_ref[:, c * FFN_CHUNK:(c + 1) * FFN_CHUNK] = (hg * jax.nn.sigmoid(hg) * hu).astype(BF16)
        emit_mixer(interleavable * (c + 1) // n_chunks)
    ffn = jnp.dot(act_ref[...], wd_ref[...], preferred_element_type=F32)
    for _ in mixer_pieces:
        pass

    x1p = jnp.concatenate([perm_ref[l] for l in range(n_slabs)], axis=1)
    out_p = _layer_norm(alpha * x1p + ffn, l2g_ref[...], l2b_ref[...])

    pitch = nat_ref.shape[1] // 8
    for l in range(n_slabs):
        for v in range(seg):
            nat_ref[l, pl.ds(v, 8, stride=pitch), :] = out_p[v * 8:(v + 1) * 8, l * LANES:(l + 1) * LANES]
    for l in range(n_slabs):
        for s in range(8):
            o_ref[s * seg:(s + 1) * seg, l * LANES:(l + 1) * LANES] = nat_ref[l, s * pitch:s * pitch + seg, :]


def _full_spec(shape, single_buffer=False):
    index_map = lambda g: (0,) * len(shape)
    if single_buffer:
        return pl.BlockSpec(shape, index_map, pipeline_mode=pl.Buffered(1))
    return pl.BlockSpec(shape, index_map)


def _layer(x, alpha, w_in, b_in, sinks, tbl, cw, cb, clg, clb, ag, cg, w_out, b_out, l1g, l1b,
           w_up, fw, fb, w_down, l2g, l2b, seq_tile):
    bsz, seq, d_model = x.shape
    d_conv = cw.shape[1]
    d_ff = w_down.shape[0]
    d_kv = N_KV_HEADS * HEAD_DIM
    n_chunks = d_ff // FFN_CHUNK
    tiles_per_row = seq // seq_tile
    n_tiles = bsz * tiles_per_row
    seg = seq_tile // 8
    nat_pitch = seg + 8
    assert seq % seq_tile == 0 and seq_tile % WINDOW == 0 and d_ff % FFN_CHUNK == 0
    assert seg % 16 == 0 and d_model % LANES == 0

    row = lambda v: v.reshape(1, -1).astype(F32)
    args = [
        x,
        w_in.astype(BF16), row(b_in), sinks.astype(F32), tbl.reshape(-1).astype(F32),
        cw.astype(F32), row(cb), row(clg), row(clb), row(ag), row(cg),
        w_out.astype(BF16), row(b_out), row(l1g), row(l1b),
        w_up.astype(BF16), fw.astype(F32), row(fb), w_down.astype(BF16), row(l2g), row(l2b),
    ]
    smem = pl.BlockSpec(memory_space=pltpu.SMEM)
    big = {1, 11, 15, 18}

    def mixer_tile(g):
        t = jnp.minimum(g, n_tiles - 1)
        return (t // tiles_per_row, t % tiles_per_row, 0)

    def ffn_tile(g):
        t = jnp.maximum(g - 1, 0)
        return (t // tiles_per_row, t % tiles_per_row, 0)

    in_specs = [pl.BlockSpec((None, seq_tile, d_model), mixer_tile)]
    for idx, a in enumerate(args[1:], start=1):
        in_specs.append(smem if idx in (3, 4) else _full_spec(a.shape, single_buffer=idx in big))

    scratch = [
        pltpu.VMEM((N_HEADS, WINDOW, WINDOW), F32),
        pltpu.VMEM((8, seq_tile + WINDOW, d_kv), BF16),
        pltpu.VMEM((seq_tile + CONV_HALO, d_conv), F32),
        pltpu.VMEM((7, seq_tile + CONV_HALO - 8, d_conv), F32),
        pltpu.VMEM((seq_tile, N_HEADS * HEAD_DIM), F32),
        pltpu.VMEM((seq_tile, d_model), BF16),
        pltpu.VMEM((seq_tile, d_model), F32),
        pltpu.VMEM((d_model // LANES, seq_tile, LANES), F32),
        pltpu.VMEM((d_model // LANES, 8 * nat_pitch, LANES), F32),
        pltpu.VMEM((seq_tile, d_ff), BF16),
        pltpu.VMEM((n_chunks, 16, FFN_CHUNK), F32),
        pltpu.VMEM((n_chunks, 16, FFN_CHUNK), F32),
    ]
    return pl.pallas_call(
        functools.partial(_layer_kernel, alpha, n_tiles, tiles_per_row),
        grid=(n_tiles + 1,),
        in_specs=in_specs,
        out_specs=pl.BlockSpec((None, seq_tile, d_model), ffn_tile),
        out_shape=jax.ShapeDtypeStruct(x.shape, x.dtype),
        scratch_shapes=scratch,
        compiler_params=pltpu.CompilerParams(
            dimension_semantics=("arbitrary",),
            vmem_limit_bytes=VMEM_LIMIT_BYTES),
        name="hybrid_layer",
    )(*args)


def kernel(x, w_in, b_in, attn_sinks, rel_bias_table, conv_dw_w, conv_dw_b, conv_ln_g, conv_ln_b,
           attn_out_gain, conv_out_gain, w_out, b_out, ln1_g, ln1_b, w_up, ffn_dw_w, ffn_dw_b, w_down,
           ln2_g, ln2_b, seq_tile=SEQ_TILE):
    depth = w_in.shape[0]
    alpha = (2.0 * depth) ** 0.25
    for l in range(depth):
        x = _layer(x, alpha, w_in[l], b_in[l], attn_sinks[l], rel_bias_table, conv_dw_w[l], conv_dw_b[l],
                   conv_ln_g[l], conv_ln_b[l], attn_out_gain[l], conv_out_gain[l], w_out[l], b_out[l],
                   ln1_g[l], ln1_b[l], w_up[l], ffn_dw_w[l], ffn_dw_b[l], w_down[l], ln2_g[l], ln2_b[l],
                   seq_tile)
    return x
```

```python
import functools
import math

import jax
import jax.numpy as jnp
from jax import lax
from jax.experimental import pallas as pl
from jax.experimental.pallas import tpu as pltpu

HEAD_DIM = 64
N_KV_HEADS = 2
GQA_GROUP = 4
N_HEADS = N_KV_HEADS * GQA_GROUP
WINDOW = 128
CONV_WIDTH = 31
N_BUCKETS = 32
MAX_DISTANCE = 128
FFN_CONV_WIDTH = 3
LN_EPS = 1e-5

SEQ_TILE = 512
CONV_HALO = 32
CONV_ROWS = 32
FFN_CHUNK = 256
LANES = 128
W_GLU, W_SHIFT, W_KV, W_Q, W_ATTN, W_CONV = 1300, 900, 500, 400, 700, 1400
STAGE_ROWS = 64
VMEM_LIMIT_BYTES = 58 * 1024 * 1024

BF16 = jnp.bfloat16
F32 = jnp.float32


def _layer_norm(r, g, b):
    mu = jnp.mean(r, axis=-1, keepdims=True)
    d = r - mu
    var = jnp.mean(d * d, axis=-1, keepdims=True)
    return d * lax.rsqrt(var + LN_EPS) * g + b


def _rms_norm(y, g):
    return y * lax.rsqrt(jnp.mean(y * y, axis=-1, keepdims=True) + LN_EPS) * g


def _dot_nt(a, b):
    return lax.dot_general(a, b, (((1,), (1,)), ((), ())), preferred_element_type=F32)


def _layer_kernel(alpha, n_tiles, tiles_per_row,
                  x_ref, w_in_hbm, b_in_ref, sinks_ref, tbl_ref, cw_ref, cb_ref, clg_ref, clb_ref,
                  ag_ref, cg_ref, w_out_hbm, b_out_ref, l1g_ref, l1b_ref,
                  w_up_hbm, fw_ref, fb_ref, wd_hbm, l2g_ref, l2b_ref,
                  o_ref,
                  w_in_ref, w_out_ref, w_up_ref, wd_ref, stage_ref, stage_sem, bias_ref, cwb_ref, kv_ref, hbuf_ref, hshift_ref, y_ref, x1_ref, perm_ref, nat_ref,
                  act_ref, cg_carry_ref, cu_carry_ref):
    ts = x_ref.shape[0]
    d_model = x_ref.shape[1]
    d_attn = N_HEADS * HEAD_DIM
    d_kv = N_KV_HEADS * HEAD_DIM
    d_conv = cw_ref.shape[1]
    d_ff = wd_ref.shape[0]
    n_chunks = d_ff // FFN_CHUNK
    seg = ts // 8
    n_slabs = d_model // LANES
    g = pl.program_id(0)
    i = jnp.minimum(g, n_tiles - 1) % tiles_per_row
    fi = jnp.maximum(g - 1, 0) % tiles_per_row

    qi = lax.broadcasted_iota(jnp.int32, (WINDOW, WINDOW), 0)
    kj = lax.broadcasted_iota(jnp.int32, (WINDOW, WINDOW), 1)
    tri = kj <= qi

    def load_weight(src_hbm, dst_ref):
        rows, cols = src_hbm.shape
        n = rows // STAGE_ROWS

        def copy(c, slot):
            return pltpu.make_async_copy(src_hbm.at[pl.ds(c * STAGE_ROWS, STAGE_ROWS), :],
                                         stage_ref.at[slot, :, pl.ds(0, cols)], stage_sem.at[slot])

        copy(0, 0).start()

        def body(c, carry):
            slot = c % 2

            @pl.when(c + 1 < n)
            def _prefetch():
                copy(c + 1, 1 - slot).start()

            copy(c, slot).wait()
            r0 = pl.multiple_of(c * STAGE_ROWS, STAGE_ROWS)
            dst_ref[pl.ds(r0, STAGE_ROWS), :] = stage_ref[slot, :, 0:cols].astype(BF16)
            return carry

        lax.fori_loop(0, n, body, 0)

    @pl.when(g == 0)
    def _first_step():
        for src_hbm, dst_ref in ((w_in_hbm, w_in_ref), (w_out_hbm, w_out_ref), (w_up_hbm, w_up_ref),
                                 (wd_hbm, wd_ref)):
            load_weight(src_hbm, dst_ref)
        x1_ref[...] = jnp.zeros(x1_ref.shape, F32)
        for k in range(CONV_WIDTH):
            cwb_ref[k] = jnp.broadcast_to(cw_ref[k:k + 1, :], (8, d_conv))
        n = (qi - kj) & (WINDOW - 1)
        max_exact = N_BUCKETS // 2
        nf = jnp.maximum(n, max_exact).astype(F32)
        large = max_exact + (jnp.log(nf / max_exact) / math.log(MAX_DISTANCE / max_exact)
                             * (N_BUCKETS - max_exact)).astype(jnp.int32)
        large = jnp.minimum(large, N_BUCKETS - 1)
        bucket = jnp.where(n < max_exact, n, large)
        for hd in range(N_HEADS):
            bias = jnp.zeros((WINDOW, WINDOW), F32)
            for bkt in range(N_BUCKETS):
                bias = jnp.where(bucket == bkt, tbl_ref[bkt * N_HEADS + hd], bias)
            bias_ref[hd] = bias

    @pl.when(i == 0)
    def _reset_mixer_history():
        kv_ref[:, 0:WINDOW, :] = jnp.zeros((8, WINDOW, d_kv), BF16)
        hbuf_ref[0:CONV_HALO, :] = jnp.zeros((CONV_HALO, d_conv), F32)

    @pl.when(fi == 0)
    def _reset_ffn_history():
        cg_carry_ref[...] = jnp.zeros(cg_carry_ref.shape, F32)
        cu_carry_ref[...] = jnp.zeros(cu_carry_ref.shape, F32)

    def mixer():
        x = x_ref[...]
        proj = jnp.dot(x.astype(BF16), w_in_ref[...], preferred_element_type=F32) + b_in_ref[...]
        q_end, k_end, v_end, a_end = d_attn, d_attn + d_kv, d_attn + 2 * d_kv, d_attn + 2 * d_kv + d_conv
        yield 0

        hbuf_ref[CONV_HALO:CONV_HALO + ts, :] = proj[:, v_end:a_end] * jax.nn.sigmoid(proj[:, a_end:])
        yield W_GLU
        for s in range(1, 8):
            hshift_ref[s - 1] = hbuf_ref[s:s + hshift_ref.shape[1], :]
            yield W_SHIFT

        lane = lax.broadcasted_iota(jnp.int32, (ts, d_kv), 1)
        lo = lane < HEAD_DIM
        for base, t in ((0, proj[:, q_end:k_end]), (4, proj[:, k_end:v_end])):
            tr = pltpu.roll(t, HEAD_DIM, axis=1)
            zero = jnp.zeros_like(t)
            kv_ref[base + 0, WINDOW:WINDOW + ts, :] = jnp.where(lo, t, zero).astype(BF16)
            kv_ref[base + 1, WINDOW:WINDOW + ts, :] = jnp.where(lo, zero, tr).astype(BF16)
            kv_ref[base + 2, WINDOW:WINDOW + ts, :] = jnp.where(lo, tr, zero).astype(BF16)
            kv_ref[base + 3, WINDOW:WINDOW + ts, :] = jnp.where(lo, zero, t).astype(BF16)
            yield W_KV
        qs = (proj[:, 0:q_end] * (HEAD_DIM ** -0.5)).astype(BF16)
        yield W_Q

        conv, attn = conv_pieces(), attention_pieces(qs)
        for w in conv:
            yield w
            yield next(conv, 0)
            yield next(attn, 0)
        for w in attn:
            yield w

        mix = jnp.dot(y_ref[...], w_out_ref[...], preferred_element_type=F32) + b_out_ref[...]
        x1_ref[...] = _layer_norm(alpha * x + mix, l1g_ref[...], l1b_ref[...])
        yield 0

    def attention_pieces(qs):
        lo_w = lax.broadcasted_iota(jnp.int32, (WINDOW, 2 * HEAD_DIM), 1) < HEAD_DIM
        for j in range(ts // WINDOW):
            r0 = j * WINDOW
            has_prev = (i > 0) | (j > 0)
            prev_mask = jnp.where(has_prev, 0.0, -jnp.inf).astype(F32)
            qb = qs[r0:r0 + WINDOW, :]
            pair_out = []
            for h in range(N_KV_HEADS):
                q2 = jnp.concatenate([qb[:, (2 * h) * 128:(2 * h + 1) * 128],
                                      qb[:, (2 * h + 1) * 128:(2 * h + 2) * 128]], axis=0)
                probs = {}
                inv = {}
                for ab in range(2):
                    keys = kv_ref[2 * h + ab, r0:r0 + 2 * WINDOW, :]
                    s = _dot_nt(q2, keys)
                    for pp in range(2):
                        hd = 4 * h + 2 * pp + ab
                        s_prev = s[pp * WINDOW:(pp + 1) * WINDOW, 0:WINDOW]
                        s_cur = s[pp * WINDOW:(pp + 1) * WINDOW, WINDOW:2 * WINDOW]
                        sc = jnp.where(tri, s_cur, s_prev + prev_mask) + bias_ref[hd]
                        sink = sinks_ref[hd]
                        m = jnp.maximum(jnp.max(sc, axis=-1, keepdims=True), sink)
                        p = jnp.exp(sc - m)
                        denom = jnp.sum(p, axis=-1, keepdims=True) + jnp.exp(sink - m)
                        zero = jnp.zeros_like(p)
                        probs[(pp, ab)] = jnp.concatenate(
                            [jnp.where(tri, zero, p), jnp.where(tri, p, zero)], axis=1).astype(BF16)
                        inv[(pp, ab)] = 1.0 / denom
                for pp in range(2):
                    va = kv_ref[4 + 2 * h + 0, r0:r0 + 2 * WINDOW, :]
                    vb = kv_ref[4 + 2 * h + 1, r0:r0 + 2 * WINDOW, :]
                    o = (jnp.dot(probs[(pp, 0)], va, preferred_element_type=F32)
                         + jnp.dot(probs[(pp, 1)], vb, preferred_element_type=F32))
                    pair_out.append(o * jnp.where(lo_w, inv[(pp, 0)], inv[(pp, 1)]))
                if h + 1 < N_KV_HEADS:
                    yield W_ATTN
            ya = jnp.concatenate(pair_out, axis=1)
            y_ref[r0:r0 + WINDOW, 0:d_attn] = _rms_norm(ya, ag_ref[...]).astype(BF16)
            yield W_ATTN
        kv_ref[:, 0:WINDOW, :] = kv_ref[:, ts:ts + WINDOW, :]

    def conv_pieces():
        for c in range(ts // CONV_ROWS):
            r0 = c * CONV_ROWS
            acc = jnp.broadcast_to(cb_ref[...], (CONV_ROWS, d_conv))
            for k in range(CONV_WIDTH):
                off = CONV_HALO - (CONV_WIDTH - 1) + k
                s, aligned = off % 8, off - off % 8
                rows = slice(r0 + aligned, r0 + aligned + CONV_ROWS)
                tap = hbuf_ref[rows, :] if s == 0 else hshift_ref[s - 1, rows, :]
                acc = acc + jnp.concatenate([cwb_ref[k]] * (CONV_ROWS // 8), axis=0) * tap
            hn = _layer_norm(acc, clg_ref[...], clb_ref[...])
            y_ref[r0:r0 + CONV_ROWS, d_attn:d_attn + d_conv] = (
                _rms_norm(hn * jax.nn.sigmoid(hn), cg_ref[...]).astype(BF16))
            yield W_CONV
        hbuf_ref[0:CONV_HALO, :] = hbuf_ref[ts:ts + CONV_HALO, :]

    after_ffn = [jnp.zeros((1, d_conv), F32)]

    def exact_zero_from(v):
        bits = lax.bitcast_convert_type(v, jnp.uint32)
        bits = lax.shift_right_logical(lax.shift_right_logical(bits, jnp.uint32(16)), jnp.uint32(16))
        return lax.bitcast_convert_type(bits, F32)

    mixer_pieces = mixer()
    interleavable = (W_GLU + 7 * W_SHIFT + 2 * W_KV + W_Q + (ts // WINDOW) * N_KV_HEADS * W_ATTN
                     + (ts // CONV_ROWS) * W_CONV)

    emitted = [0]

    def emit_mixer(target):
        while emitted[0] < target:
            emitted[0] += next(mixer_pieces)

    x1_prev = x1_ref[...]
    for l in range(n_slabs):
        for s in range(8):
            for u in range(seg // 8):
                n0 = s * seg + 8 * u
                perm_ref[l, pl.ds(seg * u + s, 8, stride=8), :] = x1_prev[n0:n0 + 8, l * LANES:(l + 1) * LANES]
    x1pb = jnp.concatenate([perm_ref[l] for l in range(n_slabs)], axis=1).astype(BF16)
    next(mixer_pieces)

    first_sublane = lax.broadcasted_iota(jnp.int32, (8, FFN_CHUNK), 0) == 0

    def causal_taps(up, carry_ref, c):
        prev = carry_ref[c]
        carry_ref[c] = up[ts - 16:ts]
        roll1 = lambda t: pltpu.roll(t, 1, axis=0)
        head1 = jnp.where(first_sublane, roll1(prev[8:16]), roll1(up[ts - 8:ts]))
        head2 = jnp.where(first_sublane, roll1(prev[0:8]), roll1(up[ts - 16:ts - 8]))
        return (jnp.concatenate([head1, up[0:ts - 8]], axis=0),
                jnp.concatenate([head2, head1, up[0:ts - 16]], axis=0))

    def neg_conv3(col0, carry_ref, c):
        cols = slice(col0 + c * FFN_CHUNK, col0 + (c + 1) * FFN_CHUNK)
        up = jnp.dot(x1pb, w_up_ref[:, cols], preferred_element_type=F32)
        up1, up2 = causal_taps(up, carry_ref, c)
        w0, w1, w2 = -fw_ref[0:1, cols], -fw_ref[1:2, cols], -fw_ref[2:3, cols]
        return w2 * up + (w1 * up1 + (w0 * up2 - fb_ref[:, cols]))

    for c in range(n_chunks):
        ng = neg_conv3(0, cg_carry_ref, c)
        nu = neg_conv3(d_ff, cu_carry_ref, c)
        act = ng * nu * (1.0 / (1.0 + jnp.exp(ng)))
        act_ref[:, c * FFN_CHUNK:(c + 1) * FFN_CHUNK] = act.astype(BF16)
        zero_row = exact_zero_from(act[ts - 8:ts, 0:LANES])[0:1, :]
        after_ffn[0] = jnp.concatenate([zero_row] * (d_conv // LANES), axis=1)
        emit_mixer(interleavable * (c + 1) // n_chunks)
    ffn = jnp.dot(act_ref[...], wd_ref[...], preferred_element_type=F32)
    for _ in mixer_pieces:
        pass

    x1p = jnp.concatenate([perm_ref[l] for l in range(n_slabs)], axis=1)
    out_p = _layer_norm(alpha * x1p + ffn, l2g_ref[...], l2b_ref[...])

    pitch = nat_ref.shape[1] // 8
    for l in range(n_slabs):
        for v in range(seg):
            nat_ref[l, pl.ds(v, 8, stride=pitch), :] = out_p[v * 8:(v + 1) * 8, l * LANES:(l + 1) * LANES]
    for l in range(n_slabs):
        for s in range(8):
            o_ref[s * seg:(s + 1) * seg, l * LANES:(l + 1) * LANES] = nat_ref[l, s * pitch:s * pitch + seg, :]


def _full_spec(shape):
    return pl.BlockSpec(shape, lambda g: (0,) * len(shape))


def _layer(x, alpha, w_in, b_in, sinks, tbl, cw, cb, clg, clb, ag, cg, w_out, b_out, l1g, l1b,
           w_up, fw, fb, w_down, l2g, l2b, seq_tile):
    bsz, seq, d_model = x.shape
    d_conv = cw.shape[1]
    d_ff = w_down.shape[0]
    d_kv = N_KV_HEADS * HEAD_DIM
    n_chunks = d_ff // FFN_CHUNK
    tiles_per_row = seq // seq_tile
    n_tiles = bsz * tiles_per_row
    seg = seq_tile // 8
    nat_pitch = seg + 8
    assert seq % seq_tile == 0 and seq_tile % WINDOW == 0 and d_ff % FFN_CHUNK == 0
    assert seg % 16 == 0 and d_model % LANES == 0

    row = lambda v: v.reshape(1, -1).astype(F32)
    weights = [w.astype(F32) for w in (w_in, w_out, w_up, w_down)]
    assert all(w.shape[0] % STAGE_ROWS == 0 and w.shape[1] % LANES == 0 for w in weights)
    args = [
        x,
        weights[0], row(b_in), sinks.astype(F32), tbl.reshape(-1).astype(F32),
        cw.astype(F32), row(cb), row(clg), row(clb), row(ag), row(cg),
        weights[1], row(b_out), row(l1g), row(l1b),
        weights[2], fw.astype(F32), row(fb), weights[3], row(l2g), row(l2b),
    ]
    smem = pl.BlockSpec(memory_space=pltpu.SMEM)
    in_hbm = pl.BlockSpec(memory_space=pl.ANY)
    big = {1, 11, 15, 18}

    def mixer_tile(g):
        t = jnp.minimum(g, n_tiles - 1)
        return (t // tiles_per_row, t % tiles_per_row, 0)

    def ffn_tile(g):
        t = jnp.maximum(g - 1, 0)
        return (t // tiles_per_row, t % tiles_per_row, 0)

    in_specs = [pl.BlockSpec((None, seq_tile, d_model), mixer_tile)]
    for idx, a in enumerate(args[1:], start=1):
        in_specs.append(smem if idx in (3, 4) else in_hbm if idx in big else _full_spec(a.shape))

    scratch = [pltpu.VMEM(w.shape, BF16) for w in weights] + [
        pltpu.VMEM((2, STAGE_ROWS, max(w.shape[1] for w in weights)), F32),
        pltpu.SemaphoreType.DMA((2,)),
        pltpu.VMEM((N_HEADS, WINDOW, WINDOW), F32),
        pltpu.VMEM((CONV_WIDTH, 8, d_conv), F32),
        pltpu.VMEM((8, seq_tile + WINDOW, d_kv), BF16),
        pltpu.VMEM((seq_tile + CONV_HALO, d_conv), F32),
        pltpu.VMEM((7, seq_tile + CONV_HALO - 8, d_conv), F32),
        pltpu.VMEM((seq_tile, d_model), BF16),
        pltpu.VMEM((seq_tile, d_model), F32),
        pltpu.VMEM((d_model // LANES, seq_tile, LANES), F32),
        pltpu.VMEM((d_model // LANES, 8 * nat_pitch, LANES), F32),
        pltpu.VMEM((seq_tile, d_ff), BF16),
        pltpu.VMEM((n_chunks, 16, FFN_CHUNK), F32),
        pltpu.VMEM((n_chunks, 16, FFN_CHUNK), F32),
    ]
    return pl.pallas_call(
        functools.partial(_layer_kernel, alpha, n_tiles, tiles_per_row),
        grid=(n_tiles + 1,),
        in_specs=in_specs,
        out_specs=pl.BlockSpec((None, seq_tile, d_model), ffn_tile),
        out_shape=jax.ShapeDtypeStruct(x.shape, x.dtype),
        scratch_shapes=scratch,
        compiler_params=pltpu.CompilerParams(
            dimension_semantics=("arbitrary",),
            vmem_limit_bytes=VMEM_LIMIT_BYTES),
        name="hybrid_layer",
    )(*args)


def kernel(x, w_in, b_in, attn_sinks, rel_bias_table, conv_dw_w, conv_dw_b, conv_ln_g, conv_ln_b,
           attn_out_gain, conv_out_gain, w_out, b_out, ln1_g, ln1_b, w_up, ffn_dw_w, ffn_dw_b, w_down,
           ln2_g, ln2_b, seq_tile=SEQ_TILE):
    depth = w_in.shape[0]
    alpha = (2.0 * depth) ** 0.25
    for l in range(depth):
        x = _layer(x, alpha, w_in[l], b_in[l], attn_sinks[l], rel_bias_table, conv_dw_w[l], conv_dw_b[l],
                   conv_ln_g[l], conv_ln_b[l], attn_out_gain[l], conv_out_gain[l], w_out[l], b_out[l],
                   ln1_g[l], ln1_b[l], w_up[l], ffn_dw_w[l], ffn_dw_b[l], w_down[l], ln2_g[l], ln2_b[l],
                   seq_tile)
    return x
```

```python
import functools
import math

import jax
import jax.numpy as jnp
from jax import lax
from jax.experimental import pallas as pl
from jax.experimental.pallas import tpu as pltpu

HEAD_DIM = 64
N_KV_HEADS = 2
GQA_GROUP = 4
N_HEADS = N_KV_HEADS * GQA_GROUP
WINDOW = 128
CONV_WIDTH = 31
N_BUCKETS = 32
MAX_DISTANCE = 128
FFN_CONV_WIDTH = 3
LN_EPS = 1e-5

SEQ_TILE = 512
CONV_HALO = 32
CONV_ROWS = 32
FFN_CHUNK = 256
LANES = 128
W_GLU, W_SHIFT, W_KV, W_Q, W_ATTN, W_CONV = 1300, 900, 500, 400, 700, 1400
STAGE_ROWS, STAGE_COLS = 256, 1792
VMEM_LIMIT_BYTES = 58 * 1024 * 1024

BF16 = jnp.bfloat16
F32 = jnp.float32


def _layer_norm(r, g, b):
    mu = jnp.mean(r, axis=-1, keepdims=True)
    d = r - mu
    var = jnp.mean(d * d, axis=-1, keepdims=True)
    return d * lax.rsqrt(var + LN_EPS) * g + b


def _rms_norm(y, g):
    return y * lax.rsqrt(jnp.mean(y * y, axis=-1, keepdims=True) + LN_EPS) * g


def _dot_nt(a, b):
    return lax.dot_general(a, b, (((1,), (1,)), ((), ())), preferred_element_type=F32)


def _layer_kernel(alpha, n_tiles, tiles_per_row,
                  x_ref, w_in_hbm, b_in_ref, sinks_ref, tbl_ref, cw_ref, cb_ref, clg_ref, clb_ref,
                  ag_ref, cg_ref, w_out_hbm, b_out_ref, l1g_ref, l1b_ref,
                  w_up_hbm, fw_ref, fb_ref, wd_hbm, l2g_ref, l2b_ref,
                  o_ref,
                  w_in_ref, w_out_ref, w_up_ref, wd_ref, stage_ref, stage_sem, bias_ref, cwb_ref, kv_ref, hbuf_ref, hshift_ref, y_ref, x1_ref, perm_ref, nat_ref,
                  act_ref, cg_carry_ref, cu_carry_ref):
    ts = x_ref.shape[0]
    d_model = x_ref.shape[1]
    d_attn = N_HEADS * HEAD_DIM
    d_kv = N_KV_HEADS * HEAD_DIM
    d_conv = cw_ref.shape[1]
    d_ff = wd_ref.shape[0]
    n_chunks = d_ff // FFN_CHUNK
    seg = ts // 8
    n_slabs = d_model // LANES
    g = pl.program_id(0)
    i = jnp.minimum(g, n_tiles - 1) % tiles_per_row
    fi = jnp.maximum(g - 1, 0) % tiles_per_row

    qi = lax.broadcasted_iota(jnp.int32, (WINDOW, WINDOW), 0)
    kj = lax.broadcasted_iota(jnp.int32, (WINDOW, WINDOW), 1)
    tri = kj <= qi

    def load_weights(pairs):
        stage_rows, stage_cols = stage_ref.shape[1:]
        blocks = []
        for src_hbm, dst_ref in pairs:
            rows, cols = src_hbm.shape
            width = max(w for w in range(LANES, stage_cols + 1, LANES) if cols % w == 0)
            blocks += [(src_hbm, dst_ref, r0, c0, width)
                       for c0 in range(0, cols, width) for r0 in range(0, rows, stage_rows)]

        def copy(k):
            src_hbm, _, r0, c0, width = blocks[k]
            return pltpu.make_async_copy(src_hbm.at[pl.ds(r0, stage_rows), pl.ds(c0, width)],
                                         stage_ref.at[k % 2, :, pl.ds(0, width)], stage_sem.at[k % 2])

        copy(0).start()
        for k, (_, dst_ref, r0, c0, width) in enumerate(blocks):
            if k + 1 < len(blocks):
                copy(k + 1).start()
            copy(k).wait()
            dst_ref[r0:r0 + stage_rows, c0:c0 + width] = stage_ref[k % 2, :, 0:width].astype(BF16)

    @pl.when(g == 0)
    def _first_step():
        load_weights(((w_in_hbm, w_in_ref), (w_out_hbm, w_out_ref), (w_up_hbm, w_up_ref), (wd_hbm, wd_ref)))
        x1_ref[...] = jnp.zeros(x1_ref.shape, F32)
        for k in range(CONV_WIDTH):
            cwb_ref[k] = jnp.broadcast_to(cw_ref[k:k + 1, :], (8, d_conv))
        n = (qi - kj) & (WINDOW - 1)
        max_exact = N_BUCKETS // 2
        nf = jnp.maximum(n, max_exact).astype(F32)
        large = max_exact + (jnp.log(nf / max_exact) / math.log(MAX_DISTANCE / max_exact)
                             * (N_BUCKETS - max_exact)).astype(jnp.int32)
        large = jnp.minimum(large, N_BUCKETS - 1)
        bucket = jnp.where(n < max_exact, n, large)
        for hd in range(N_HEADS):
            bias = jnp.zeros((WINDOW, WINDOW), F32)
            for bkt in range(N_BUCKETS):
                bias = jnp.where(bucket == bkt, tbl_ref[bkt * N_HEADS + hd], bias)
            bias_ref[hd] = bias

    @pl.when(i == 0)
    def _reset_mixer_history():
        kv_ref[:, 0:WINDOW, :] = jnp.zeros((8, WINDOW, d_kv), BF16)
        hbuf_ref[0:CONV_HALO, :] = jnp.zeros((CONV_HALO, d_conv), F32)

    @pl.when(fi == 0)
    def _reset_ffn_history():
        cg_carry_ref[...] = jnp.zeros(cg_carry_ref.shape, F32)
        cu_carry_ref[...] = jnp.zeros(cu_carry_ref.shape, F32)

    def mixer():
        x = x_ref[...]
        proj = jnp.dot(x.astype(BF16), w_in_ref[...], preferred_element_type=F32) + b_in_ref[...]
        q_end, k_end, v_end, a_end = d_attn, d_attn + d_kv, d_attn + 2 * d_kv, d_attn + 2 * d_kv + d_conv
        yield 0

        hbuf_ref[CONV_HALO:CONV_HALO + ts, :] = proj[:, v_end:a_end] * jax.nn.sigmoid(proj[:, a_end:])
        yield W_GLU
        for s in range(1, 8):
            hshift_ref[s - 1] = hbuf_ref[s:s + hshift_ref.shape[1], :]
            yield W_SHIFT

        lane = lax.broadcasted_iota(jnp.int32, (ts, d_kv), 1)
        lo = lane < HEAD_DIM
        for base, t in ((0, proj[:, q_end:k_end]), (4, proj[:, k_end:v_end])):
            tr = pltpu.roll(t, HEAD_DIM, axis=1)
            zero = jnp.zeros_like(t)
            kv_ref[base + 0, WINDOW:WINDOW + ts, :] = jnp.where(lo, t, zero).astype(BF16)
            kv_ref[base + 1, WINDOW:WINDOW + ts, :] = jnp.where(lo, zero, tr).astype(BF16)
            kv_ref[base + 2, WINDOW:WINDOW + ts, :] = jnp.where(lo, tr, zero).astype(BF16)
            kv_ref[base + 3, WINDOW:WINDOW + ts, :] = jnp.where(lo, zero, t).astype(BF16)
            yield W_KV
        qs = (proj[:, 0:q_end] * (HEAD_DIM ** -0.5)).astype(BF16)
        yield W_Q

        conv, attn = conv_pieces(), attention_pieces(qs)
        for w in conv:
            yield w
            yield next(conv, 0)
            yield next(attn, 0)
        for w in attn:
            yield w

        mix = jnp.dot(y_ref[...], w_out_ref[...], preferred_element_type=F32) + b_out_ref[...]
        x1_ref[...] = _layer_norm(alpha * x + mix, l1g_ref[...], l1b_ref[...])
        yield 0

    def attention_pieces(qs):
        lo_w = lax.broadcasted_iota(jnp.int32, (WINDOW, 2 * HEAD_DIM), 1) < HEAD_DIM
        for j in range(ts // WINDOW):
            r0 = j * WINDOW
            has_prev = (i > 0) | (j > 0)
            prev_mask = jnp.where(has_prev, 0.0, -jnp.inf).astype(F32)
            qb = qs[r0:r0 + WINDOW, :]
            pair_out = []
            for h in range(N_KV_HEADS):
                q2 = jnp.concatenate([qb[:, (2 * h) * 128:(2 * h + 1) * 128],
                                      qb[:, (2 * h + 1) * 128:(2 * h + 2) * 128]], axis=0)
                probs = {}
                inv = {}
                for ab in range(2):
                    keys = kv_ref[2 * h + ab, r0:r0 + 2 * WINDOW, :]
                    s = _dot_nt(q2, keys)
                    for pp in range(2):
                        hd = 4 * h + 2 * pp + ab
                        s_prev = s[pp * WINDOW:(pp + 1) * WINDOW, 0:WINDOW]
                        s_cur = s[pp * WINDOW:(pp + 1) * WINDOW, WINDOW:2 * WINDOW]
                        sc = jnp.where(tri, s_cur, s_prev + prev_mask) + bias_ref[hd]
                        sink = sinks_ref[hd]
                        m = jnp.maximum(jnp.max(sc, axis=-1, keepdims=True), sink)
                        p = jnp.exp(sc - m)
                        denom = jnp.sum(p, axis=-1, keepdims=True) + jnp.exp(sink - m)
                        zero = jnp.zeros_like(p)
                        probs[(pp, ab)] = jnp.concatenate(
                            [jnp.where(tri, zero, p), jnp.where(tri, p, zero)], axis=1).astype(BF16)
                        inv[(pp, ab)] = 1.0 / denom
                for pp in range(2):
                    va = kv_ref[4 + 2 * h + 0, r0:r0 + 2 * WINDOW, :]
                    vb = kv_ref[4 + 2 * h + 1, r0:r0 + 2 * WINDOW, :]
                    o = (jnp.dot(probs[(pp, 0)], va, preferred_element_type=F32)
                         + jnp.dot(probs[(pp, 1)], vb, preferred_element_type=F32))
                    pair_out.append(o * jnp.where(lo_w, inv[(pp, 0)], inv[(pp, 1)]))
                if h + 1 < N_KV_HEADS:
                    yield W_ATTN
            ya = jnp.concatenate(pair_out, axis=1)
            y_ref[r0:r0 + WINDOW, 0:d_attn] = _rms_norm(ya, ag_ref[...]).astype(BF16)
            yield W_ATTN
        kv_ref[:, 0:WINDOW, :] = kv_ref[:, ts:ts + WINDOW, :]

    def conv_pieces():
        for c in range(ts // CONV_ROWS):
            r0 = c * CONV_ROWS
            acc = jnp.broadcast_to(cb_ref[...], (CONV_ROWS, d_conv))
            for k in range(CONV_WIDTH):
                off = CONV_HALO - (CONV_WIDTH - 1) + k
                s, aligned = off % 8, off - off % 8
                rows = slice(r0 + aligned, r0 + aligned + CONV_ROWS)
                tap = hbuf_ref[rows, :] if s == 0 else hshift_ref[s - 1, rows, :]
                acc = acc + jnp.concatenate([cwb_ref[k]] * (CONV_ROWS // 8), axis=0) * tap
            hn = _layer_norm(acc, clg_ref[...], clb_ref[...])
            y_ref[r0:r0 + CONV_ROWS, d_attn:d_attn + d_conv] = (
                _rms_norm(hn * jax.nn.sigmoid(hn), cg_ref[...]).astype(BF16))
            yield W_CONV
        hbuf_ref[0:CONV_HALO, :] = hbuf_ref[ts:ts + CONV_HALO, :]

    after_ffn = [jnp.zeros((1, d_conv), F32)]

    def exact_zero_from(v):
        bits = lax.bitcast_convert_type(v, jnp.uint32)
        bits = lax.shift_right_logical(lax.shift_right_logical(bits, jnp.uint32(16)), jnp.uint32(16))
        return lax.bitcast_convert_type(bits, F32)

    mixer_pieces = mixer()
    interleavable = (W_GLU + 7 * W_SHIFT + 2 * W_KV + W_Q + (ts // WINDOW) * N_KV_HEADS * W_ATTN
                     + (ts // CONV_ROWS) * W_CONV)

    emitted = [0]

    def emit_mixer(target):
        while emitted[0] < target:
            emitted[0] += next(mixer_pieces)

    x1_prev = x1_ref[...]
    for l in range(n_slabs):
        for s in range(8):
            for u in range(seg // 8):
                n0 = s * seg + 8 * u
                perm_ref[l, pl.ds(seg * u + s, 8, stride=8), :] = x1_prev[n0:n0 + 8, l * LANES:(l + 1) * LANES]
    x1pb = jnp.concatenate([perm_ref[l] for l in range(n_slabs)], axis=1).astype(BF16)
    next(mixer_pieces)

    first_sublane = lax.broadcasted_iota(jnp.int32, (8, FFN_CHUNK), 0) == 0

    def causal_taps(up, carry_ref, c):
        prev = carry_ref[c]
        carry_ref[c] = up[ts - 16:ts]
        roll1 = lambda t: pltpu.roll(t, 1, axis=0)
        head1 = jnp.where(first_sublane, roll1(prev[8:16]), roll1(up[ts - 8:ts]))
        head2 = jnp.where(first_sublane, roll1(prev[0:8]), roll1(up[ts - 16:ts - 8]))
        return (jnp.concatenate([head1, up[0:ts - 8]], axis=0),
                jnp.concatenate([head2, head1, up[0:ts - 16]], axis=0))

    def neg_conv3(col0, carry_ref, c):
        cols = slice(col0 + c * FFN_CHUNK, col0 + (c + 1) * FFN_CHUNK)
        up = jnp.dot(x1pb, w_up_ref[:, cols], preferred_element_type=F32)
        up1, up2 = causal_taps(up, carry_ref, c)
        w0, w1, w2 = -fw_ref[0:1, cols], -fw_ref[1:2, cols], -fw_ref[2:3, cols]
        return w2 * up + (w1 * up1 + (w0 * up2 - fb_ref[:, cols]))

    for c in range(n_chunks):
        ng = neg_conv3(0, cg_carry_ref, c)
        nu = neg_conv3(d_ff, cu_carry_ref, c)
        act = ng * nu * (1.0 / (1.0 + jnp.exp(ng)))
        act_ref[:, c * FFN_CHUNK:(c + 1) * FFN_CHUNK] = act.astype(BF16)
        zero_row = exact_zero_from(act[ts - 8:ts, 0:LANES])[0:1, :]
        after_ffn[0] = jnp.concatenate([zero_row] * (d_conv // LANES), axis=1)
        emit_mixer(interleavable * (c + 1) // n_chunks)
    ffn = jnp.dot(act_ref[...], wd_ref[...], preferred_element_type=F32)
    for _ in mixer_pieces:
        pass

    x1p = jnp.concatenate([perm_ref[l] for l in range(n_slabs)], axis=1)
    out_p = _layer_norm(alpha * x1p + ffn, l2g_ref[...], l2b_ref[...])

    pitch = nat_ref.shape[1] // 8
    for l in range(n_slabs):
        for v in range(seg):
            nat_ref[l, pl.ds(v, 8, stride=pitch), :] = out_p[v * 8:(v + 1) * 8, l * LANES:(l + 1) * LANES]
    for l in range(n_slabs):
        for s in range(8):
            o_ref[s * seg:(s + 1) * seg, l * LANES:(l + 1) * LANES] = nat_ref[l, s * pitch:s * pitch + seg, :]


def _full_spec(shape):
    return pl.BlockSpec(shape, lambda g: (0,) * len(shape))


def _layer(x, alpha, w_in, b_in, sinks, tbl, cw, cb, clg, clb, ag, cg, w_out, b_out, l1g, l1b,
           w_up, fw, fb, w_down, l2g, l2b, seq_tile):
    bsz, seq, d_model = x.shape
    d_conv = cw.shape[1]
    d_ff = w_down.shape[0]
    d_kv = N_KV_HEADS * HEAD_DIM
    n_chunks = d_ff // FFN_CHUNK
    tiles_per_row = seq // seq_tile
    n_tiles = bsz * tiles_per_row
    seg = seq_tile // 8
    nat_pitch = seg + 8
    assert seq % seq_tile == 0 and seq_tile % WINDOW == 0 and d_ff % FFN_CHUNK == 0
    assert seg % 16 == 0 and d_model % LANES == 0

    row = lambda v: v.reshape(1, -1).astype(F32)
    weights = [w.astype(F32) for w in (w_in, w_out, w_up, w_down)]
    assert all(w.shape[0] % STAGE_ROWS == 0 and w.shape[1] % LANES == 0 for w in weights)
    args = [
        x,
        weights[0], row(b_in), sinks.astype(F32), tbl.reshape(-1).astype(F32),
        cw.astype(F32), row(cb), row(clg), row(clb), row(ag), row(cg),
        weights[1], row(b_out), row(l1g), row(l1b),
        weights[2], fw.astype(F32), row(fb), weights[3], row(l2g), row(l2b),
    ]
    smem = pl.BlockSpec(memory_space=pltpu.SMEM)
    in_hbm = pl.BlockSpec(memory_space=pl.ANY)
    big = {1, 11, 15, 18}

    def mixer_tile(g):
        t = jnp.minimum(g, n_tiles - 1)
        return (t // tiles_per_row, t % tiles_per_row, 0)

    def ffn_tile(g):
        t = jnp.maximum(g - 1, 0)
        return (t // tiles_per_row, t % tiles_per_row, 0)

    in_specs = [pl.BlockSpec((None, seq_tile, d_model), mixer_tile)]
    for idx, a in enumerate(args[1:], start=1):
        in_specs.append(smem if idx in (3, 4) else in_hbm if idx in big else _full_spec(a.shape))

    scratch = [pltpu.VMEM(w.shape, BF16) for w in weights] + [
        pltpu.VMEM((2, STAGE_ROWS, STAGE_COLS), F32),
        pltpu.SemaphoreType.DMA((2,)),
        pltpu.VMEM((N_HEADS, WINDOW, WINDOW), F32),
        pltpu.VMEM((CONV_WIDTH, 8, d_conv), F32),
        pltpu.VMEM((8, seq_tile + WINDOW, d_kv), BF16),
        pltpu.VMEM((seq_tile + CONV_HALO, d_conv), F32),
        pltpu.VMEM((7, seq_tile + CONV_HALO - 8, d_conv), F32),
        pltpu.VMEM((seq_tile, d_model), BF16),
        pltpu.VMEM((seq_tile, d_model), F32),
        pltpu.VMEM((d_model // LANES, seq_tile, LANES), F32),
        pltpu.VMEM((d_model // LANES, 8 * nat_pitch, LANES), F32),
        pltpu.VMEM((seq_tile, d_ff), BF16),
        pltpu.VMEM((n_chunks, 16, FFN_CHUNK), F32),
        pltpu.VMEM((n_chunks, 16, FFN_CHUNK), F32),
    ]
    return pl.pallas_call(
        functools.partial(_layer_kernel, alpha, n_tiles, tiles_per_row),
        grid=(n_tiles + 1,),
        in_specs=in_specs,
        out_specs=pl.BlockSpec((None, seq_tile, d_model), ffn_tile),
        out_shape=jax.ShapeDtypeStruct(x.shape, x.dtype),
        scratch_shapes=scratch,
        compiler_params=pltpu.CompilerParams(
            dimension_semantics=("arbitrary",),
            vmem_limit_bytes=VMEM_LIMIT_BYTES),
        name="hybrid_layer",
    )(*args)


def kernel(x, w_in, b_in, attn_sinks, rel_bias_table, conv_dw_w, conv_dw_b, conv_ln_g, conv_ln_b,
           attn_out_gain, conv_out_gain, w_out, b_out, ln1_g, ln1_b, w_up, ffn_dw_w, ffn_dw_b, w_down,
           ln2_g, ln2_b, seq_tile=SEQ_TILE):
    depth = w_in.shape[0]
    alpha = (2.0 * depth) ** 0.25
    for l in range(depth):
        x = _layer(x, alpha, w_in[l], b_in[l], attn_sinks[l], rel_bias_table, conv_dw_w[l], conv_dw_b[l],
                   conv_ln_g[l], conv_ln_b[l], attn_out_gain[l], conv_out_gain[l], w_out[l], b_out[l],
                   ln1_g[l], ln1_b[l], w_up[l], ffn_dw_w[l], ffn_dw_b[l], w_down[l], ln2_g[l], ln2_b[l],
                   seq_tile)
    return x
```

```python
import functools
import math

import jax
import jax.numpy as jnp
from jax import lax
from jax.experimental import pallas as pl
from jax.experimental.pallas import tpu as pltpu

HEAD_DIM = 64
N_KV_HEADS = 2
GQA_GROUP = 4
N_HEADS = N_KV_HEADS * GQA_GROUP
WINDOW = 128
CONV_WIDTH = 31
N_BUCKETS = 32
MAX_DISTANCE = 128
FFN_CONV_WIDTH = 3
LN_EPS = 1e-5

SEQ_TILE = 512
CONV_HALO = 32
CONV_ROWS = 32
FFN_CHUNK = 256
LANES = 128
W_GLU, W_SHIFT, W_KV, W_Q, W_ATTN, W_CONV = 1300, 900, 500, 400, 700, 1400
STAGE_ROWS, STAGE_COLS = 256, 1792
VMEM_LIMIT_BYTES = 58 * 1024 * 1024

BF16 = jnp.bfloat16
F32 = jnp.float32


def _layer_norm(r, g, b):
    mu = jnp.mean(r, axis=-1, keepdims=True)
    d = r - mu
    var = jnp.mean(d * d, axis=-1, keepdims=True)
    return d * lax.rsqrt(var + LN_EPS) * g + b


def _rms_norm(y, g):
    return y * lax.rsqrt(jnp.mean(y * y, axis=-1, keepdims=True) + LN_EPS) * g


def _dot_nt(a, b):
    return lax.dot_general(a, b, (((1,), (1,)), ((), ())), preferred_element_type=F32)


def _layer_kernel(alpha, n_tiles, tiles_per_row,
                  x_ref, w_in_hbm, b_in_ref, sinks_ref, tbl_ref, cw_ref, cb_ref, clg_ref, clb_ref,
                  ag_ref, cg_ref, w_out_hbm, b_out_ref, l1g_ref, l1b_ref,
                  w_up_hbm, fw_ref, fb_ref, wd_hbm, l2g_ref, l2b_ref,
                  o_ref,
                  w_in_ref, w_out_ref, w_up_ref, wd_ref, stage_ref, stage_sem, bias_ref, cwb_ref, kv_ref, hbuf_ref, hshift_ref, y_ref, x1_ref, perm_ref, nat_ref,
                  act_ref, cg_carry_ref, cu_carry_ref):
    ts = x_ref.shape[0]
    d_model = x_ref.shape[1]
    d_attn = N_HEADS * HEAD_DIM
    d_kv = N_KV_HEADS * HEAD_DIM
    d_conv = cw_ref.shape[1]
    d_ff = wd_ref.shape[0]
    n_chunks = d_ff // FFN_CHUNK
    n_slabs = d_model // LANES
    g = pl.program_id(0)
    i = jnp.minimum(g, n_tiles - 1) % tiles_per_row
    fi = jnp.maximum(g - 1, 0) % tiles_per_row

    qi = lax.broadcasted_iota(jnp.int32, (WINDOW, WINDOW), 0)
    kj = lax.broadcasted_iota(jnp.int32, (WINDOW, WINDOW), 1)
    tri = kj <= qi

    def load_weights(pairs):
        stage_rows, stage_cols = stage_ref.shape[1:]
        blocks = []
        for src_hbm, dst_ref in pairs:
            rows, cols = src_hbm.shape
            width = max(w for w in range(LANES, stage_cols + 1, LANES) if cols % w == 0)
            blocks += [(src_hbm, dst_ref, r0, c0, width)
                       for c0 in range(0, cols, width) for r0 in range(0, rows, stage_rows)]

        def copy(k):
            src_hbm, _, r0, c0, width = blocks[k]
            return pltpu.make_async_copy(src_hbm.at[pl.ds(r0, stage_rows), pl.ds(c0, width)],
                                         stage_ref.at[k % 2, :, pl.ds(0, width)], stage_sem.at[k % 2])

        copy(0).start()
        for k, (_, dst_ref, r0, c0, width) in enumerate(blocks):
            if k + 1 < len(blocks):
                copy(k + 1).start()
            copy(k).wait()
            dst_ref[r0:r0 + stage_rows, c0:c0 + width] = stage_ref[k % 2, :, 0:width].astype(BF16)

    @pl.when(g == 0)
    def _first_step():
        load_weights(((w_in_hbm, w_in_ref), (w_out_hbm, w_out_ref), (w_up_hbm, w_up_ref), (wd_hbm, wd_ref)))
        x1_ref[...] = jnp.zeros(x1_ref.shape, F32)
        for k in range(CONV_WIDTH):
            cwb_ref[k] = jnp.broadcast_to(cw_ref[k:k + 1, :], (8, d_conv))
        n = (qi - kj) & (WINDOW - 1)
        max_exact = N_BUCKETS // 2
        nf = jnp.maximum(n, max_exact).astype(F32)
        large = max_exact + (jnp.log(nf / max_exact) / math.log(MAX_DISTANCE / max_exact)
                             * (N_BUCKETS - max_exact)).astype(jnp.int32)
        large = jnp.minimum(large, N_BUCKETS - 1)
        bucket = jnp.where(n < max_exact, n, large)
        for hd in range(N_HEADS):
            bias = jnp.zeros((WINDOW, WINDOW), F32)
            for bkt in range(N_BUCKETS):
                bias = jnp.where(bucket == bkt, tbl_ref[bkt * N_HEADS + hd], bias)
            bias_ref[hd] = bias

    @pl.when(i == 0)
    def _reset_mixer_history():
        kv_ref[:, 0:WINDOW, :] = jnp.zeros((8, WINDOW, d_kv), BF16)
        hbuf_ref[0:CONV_HALO, :] = jnp.zeros((CONV_HALO, d_conv), F32)

    @pl.when(fi == 0)
    def _reset_ffn_history():
        cg_carry_ref[...] = jnp.zeros(cg_carry_ref.shape, F32)
        cu_carry_ref[...] = jnp.zeros(cu_carry_ref.shape, F32)

    def mixer():
        x = x_ref[...]
        proj = jnp.dot(x.astype(BF16), w_in_ref[...], preferred_element_type=F32) + b_in_ref[...]
        q_end, k_end, v_end, a_end = d_attn, d_attn + d_kv, d_attn + 2 * d_kv, d_attn + 2 * d_kv + d_conv
        yield 0

        hbuf_ref[CONV_HALO:CONV_HALO + ts, :] = proj[:, v_end:a_end] * jax.nn.sigmoid(proj[:, a_end:])
        yield W_GLU
        for s in range(1, 8):
            hshift_ref[s - 1] = hbuf_ref[s:s + hshift_ref.shape[1], :]
            yield W_SHIFT

        lane = lax.broadcasted_iota(jnp.int32, (ts, d_kv), 1)
        lo = lane < HEAD_DIM
        for base, t in ((0, proj[:, q_end:k_end]), (4, proj[:, k_end:v_end])):
            tr = pltpu.roll(t, HEAD_DIM, axis=1)
            zero = jnp.zeros_like(t)
            kv_ref[base + 0, WINDOW:WINDOW + ts, :] = jnp.where(lo, t, zero).astype(BF16)
            kv_ref[base + 1, WINDOW:WINDOW + ts, :] = jnp.where(lo, zero, tr).astype(BF16)
            kv_ref[base + 2, WINDOW:WINDOW + ts, :] = jnp.where(lo, tr, zero).astype(BF16)
            kv_ref[base + 3, WINDOW:WINDOW + ts, :] = jnp.where(lo, zero, t).astype(BF16)
            yield W_KV
        qs = (proj[:, 0:q_end] * (HEAD_DIM ** -0.5)).astype(BF16)
        yield W_Q

        conv, attn = conv_pieces(), attention_pieces(qs)
        for w in conv:
            yield w
            yield next(conv, 0)
            yield next(attn, 0)
        for w in attn:
            yield w

        mix = jnp.dot(y_ref[...], w_out_ref[...], preferred_element_type=F32) + b_out_ref[...]
        x1_ref[...] = _layer_norm(alpha * x + mix, l1g_ref[...], l1b_ref[...])
        yield 0

    def attention_pieces(qs):
        lo_w = lax.broadcasted_iota(jnp.int32, (WINDOW, 2 * HEAD_DIM), 1) < HEAD_DIM
        for j in range(ts // WINDOW):
            r0 = j * WINDOW
            has_prev = (i > 0) | (j > 0)
            prev_mask = jnp.where(has_prev, 0.0, -jnp.inf).astype(F32)
            qb = qs[r0:r0 + WINDOW, :]
            pair_out = []
            for h in range(N_KV_HEADS):
                q2 = jnp.concatenate([qb[:, (2 * h) * LANES:(2 * h + 1) * LANES],
                                      qb[:, (2 * h + 1) * LANES:(2 * h + 2) * LANES]], axis=0)
                probs = {}
                inv = {}
                for ab in range(2):
                    keys = kv_ref[2 * h + ab, r0:r0 + 2 * WINDOW, :]
                    s = _dot_nt(q2, keys)
                    for pp in range(2):
                        hd = 4 * h + 2 * pp + ab
                        s_prev = s[pp * WINDOW:(pp + 1) * WINDOW, 0:WINDOW]
                        s_cur = s[pp * WINDOW:(pp + 1) * WINDOW, WINDOW:2 * WINDOW]
                        sc = jnp.where(tri, s_cur, s_prev + prev_mask) + bias_ref[hd]
                        sink = sinks_ref[hd]
                        m = jnp.maximum(jnp.max(sc, axis=-1, keepdims=True), sink)
                        p = jnp.exp(sc - m)
                        denom = jnp.sum(p, axis=-1, keepdims=True) + jnp.exp(sink - m)
                        zero = jnp.zeros_like(p)
                        probs[(pp, ab)] = jnp.concatenate(
                            [jnp.where(tri, zero, p), jnp.where(tri, p, zero)], axis=1).astype(BF16)
                        inv[(pp, ab)] = 1.0 / denom
                for pp in range(2):
                    va = kv_ref[4 + 2 * h + 0, r0:r0 + 2 * WINDOW, :]
                    vb = kv_ref[4 + 2 * h + 1, r0:r0 + 2 * WINDOW, :]
                    o = (jnp.dot(probs[(pp, 0)], va, preferred_element_type=F32)
                         + jnp.dot(probs[(pp, 1)], vb, preferred_element_type=F32))
                    pair_out.append(o * jnp.where(lo_w, inv[(pp, 0)], inv[(pp, 1)]))
                if h + 1 < N_KV_HEADS:
                    yield W_ATTN
            ya = jnp.concatenate(pair_out, axis=1)
            y_ref[r0:r0 + WINDOW, 0:d_attn] = _rms_norm(ya, ag_ref[...]).astype(BF16)
            yield W_ATTN
        kv_ref[:, 0:WINDOW, :] = kv_ref[:, ts:ts + WINDOW, :]

    def conv_pieces():
        for c in range(ts // CONV_ROWS):
            r0 = c * CONV_ROWS
            acc = jnp.broadcast_to(cb_ref[...], (CONV_ROWS, d_conv))
            for k in range(CONV_WIDTH):
                off = CONV_HALO - (CONV_WIDTH - 1) + k
                s, aligned = off % 8, off - off % 8
                rows = slice(r0 + aligned, r0 + aligned + CONV_ROWS)
                tap = hbuf_ref[rows, :] if s == 0 else hshift_ref[s - 1, rows, :]
                acc = acc + jnp.concatenate([cwb_ref[k]] * (CONV_ROWS // 8), axis=0) * tap
            hn = _layer_norm(acc, clg_ref[...], clb_ref[...])
            y_ref[r0:r0 + CONV_ROWS, d_attn:d_attn + d_conv] = (
                _rms_norm(hn * jax.nn.sigmoid(hn), cg_ref[...]).astype(BF16))
            yield W_CONV
        hbuf_ref[0:CONV_HALO, :] = hbuf_ref[ts:ts + CONV_HALO, :]

    mixer_pieces = mixer()
    interleavable = (W_GLU + 7 * W_SHIFT + 2 * W_KV + W_Q + (ts // WINDOW) * N_KV_HEADS * W_ATTN
                     + (ts // CONV_ROWS) * W_CONV)

    emitted = [0]

    def emit_mixer(target):
        while emitted[0] < target:
            emitted[0] += next(mixer_pieces)

    seg = ts // 8
    x1_prev = x1_ref[...]
    for l in range(n_slabs):
        for s in range(8):
            for u in range(seg // 8):
                n0 = s * seg + 8 * u
                perm_ref[l, pl.ds(seg * u + s, 8, stride=8), :] = x1_prev[n0:n0 + 8, l * LANES:(l + 1) * LANES]
    x1pb = jnp.concatenate([perm_ref[l] for l in range(n_slabs)], axis=1).astype(BF16)
    next(mixer_pieces)

    first_sublane = lax.broadcasted_iota(jnp.int32, (8, FFN_CHUNK), 0) == 0

    def causal_taps(up, carry_ref, c):
        prev = carry_ref[c]
        carry_ref[c] = up[ts - 16:ts]
        roll1 = lambda t: pltpu.roll(t, 1, axis=0)
        head1 = jnp.where(first_sublane, roll1(prev[8:16]), roll1(up[ts - 8:ts]))
        head2 = jnp.where(first_sublane, roll1(prev[0:8]), roll1(up[ts - 16:ts - 8]))
        return (jnp.concatenate([head1, up[0:ts - 8]], axis=0),
                jnp.concatenate([head2, head1, up[0:ts - 16]], axis=0))

    def neg_conv3(col0, carry_ref, c):
        cols = slice(col0 + c * FFN_CHUNK, col0 + (c + 1) * FFN_CHUNK)
        up = jnp.dot(x1pb, w_up_ref[:, cols], preferred_element_type=F32)
        up1, up2 = causal_taps(up, carry_ref, c)
        w0, w1, w2 = -fw_ref[0:1, cols], -fw_ref[1:2, cols], -fw_ref[2:3, cols]
        return w2 * up + (w1 * up1 + (w0 * up2 - fb_ref[:, cols]))

    for c in range(n_chunks):
        ng = neg_conv3(0, cg_carry_ref, c)
        nu = neg_conv3(d_ff, cu_carry_ref, c)
        act = ng * nu * (1.0 / (1.0 + jnp.exp(ng)))
        act_ref[:, c * FFN_CHUNK:(c + 1) * FFN_CHUNK] = act.astype(BF16)
        emit_mixer(interleavable * (c + 1) // n_chunks)
    ffn = jnp.dot(act_ref[...], wd_ref[...], preferred_element_type=F32)
    for _ in mixer_pieces:
        pass

    x1p = jnp.concatenate([perm_ref[l] for l in range(n_slabs)], axis=1)
    out_p = _layer_norm(alpha * x1p + ffn, l2g_ref[...], l2b_ref[...])

    pitch = nat_ref.shape[1] // 8
    for l in range(n_slabs):
        for v in range(seg):
            nat_ref[l, pl.ds(v, 8, stride=pitch), :] = out_p[v * 8:(v + 1) * 8, l * LANES:(l + 1) * LANES]
    for l in range(n_slabs):
        for s in range(8):
            o_ref[s * seg:(s + 1) * seg, l * LANES:(l + 1) * LANES] = nat_ref[l, s * pitch:s * pitch + seg, :]


def _full_spec(shape):
    return pl.BlockSpec(shape, lambda g: (0,) * len(shape))


def _layer(x, alpha, w_in, b_in, sinks, tbl, cw, cb, clg, clb, ag, cg, w_out, b_out, l1g, l1b,
           w_up, fw, fb, w_down, l2g, l2b, seq_tile):
    bsz, seq, d_model = x.shape
    d_conv = cw.shape[1]
    d_ff = w_down.shape[0]
    d_kv = N_KV_HEADS * HEAD_DIM
    n_chunks = d_ff // FFN_CHUNK
    tiles_per_row = seq // seq_tile
    n_tiles = bsz * tiles_per_row
    seg = seq_tile // 8
    nat_pitch = seg + 8
    assert seq % seq_tile == 0 and seq_tile % WINDOW == 0 and d_ff % FFN_CHUNK == 0
    assert seg % 16 == 0 and d_model % LANES == 0

    row = lambda v: v.reshape(1, -1).astype(F32)
    weights = [w.astype(F32) for w in (w_in, w_out, w_up, w_down)]
    assert all(w.shape[0] % STAGE_ROWS == 0 and w.shape[1] % LANES == 0 for w in weights)
    args = [
        x,
        weights[0], row(b_in), sinks.astype(F32), tbl.reshape(-1).astype(F32),
        cw.astype(F32), row(cb), row(clg), row(clb), row(ag), row(cg),
        weights[1], row(b_out), row(l1g), row(l1b),
        weights[2], fw.astype(F32), row(fb), weights[3], row(l2g), row(l2b),
    ]
    smem = pl.BlockSpec(memory_space=pltpu.SMEM)
    in_hbm = pl.BlockSpec(memory_space=pl.ANY)
    big = {1, 11, 15, 18}

    def mixer_tile(g):
        t = jnp.minimum(g, n_tiles - 1)
        return (t // tiles_per_row, t % tiles_per_row, 0)

    def ffn_tile(g):
        t = jnp.maximum(g - 1, 0)
        return (t // tiles_per_row, t % tiles_per_row, 0)

    in_specs = [pl.BlockSpec((None, seq_tile, d_model), mixer_tile)]
    for idx, a in enumerate(args[1:], start=1):
        in_specs.append(smem if idx in (3, 4) else in_hbm if idx in big else _full_spec(a.shape))

    scratch = [pltpu.VMEM(w.shape, BF16) for w in weights] + [
        pltpu.VMEM((2, STAGE_ROWS, STAGE_COLS), F32),
        pltpu.SemaphoreType.DMA((2,)),
        pltpu.VMEM((N_HEADS, WINDOW, WINDOW), F32),
        pltpu.VMEM((CONV_WIDTH, 8, d_conv), F32),
        pltpu.VMEM((8, seq_tile + WINDOW, d_kv), BF16),
        pltpu.VMEM((seq_tile + CONV_HALO, d_conv), F32),
        pltpu.VMEM((7, seq_tile + CONV_HALO - 8, d_conv), F32),
        pltpu.VMEM((seq_tile, d_model), BF16),
        pltpu.VMEM((seq_tile, d_model), F32),
        pltpu.VMEM((d_model // LANES, seq_tile, LANES), F32),
        pltpu.VMEM((d_model // LANES, 8 * nat_pitch, LANES), F32),
        pltpu.VMEM((seq_tile, d_ff), BF16),
        pltpu.VMEM((n_chunks, 16, FFN_CHUNK), F32),
        pltpu.VMEM((n_chunks, 16, FFN_CHUNK), F32),
    ]
    return pl.pallas_call(
        functools.partial(_layer_kernel, alpha, n_tiles, tiles_per_row),
        grid=(n_tiles + 1,),
        in_specs=in_specs,
        out_specs=pl.BlockSpec((None, seq_tile, d_model), ffn_tile),
        out_shape=jax.ShapeDtypeStruct(x.shape, x.dtype),
        scratch_shapes=scratch,
        compiler_params=pltpu.CompilerParams(
            dimension_semantics=("arbitrary",),
            vmem_limit_bytes=VMEM_LIMIT_BYTES),
        name="hybrid_layer",
    )(*args)


def kernel(x, w_in, b_in, attn_sinks, rel_bias_table, conv_dw_w, conv_dw_b, conv_ln_g, conv_ln_b,
           attn_out_gain, conv_out_gain, w_out, b_out, ln1_g, ln1_b, w_up, ffn_dw_w, ffn_dw_b, w_down,
           ln2_g, ln2_b, seq_tile=SEQ_TILE):
    depth = w_in.shape[0]
    alpha = (2.0 * depth) ** 0.25
    for l in range(depth):
        x = _layer(x, alpha, w_in[l], b_in[l], attn_sinks[l], rel_bias_table, conv_dw_w[l], conv_dw_b[l],
                   conv_ln_g[l], conv_ln_b[l], attn_out_gain[l], conv_out_gain[l], w_out[l], b_out[l],
                   ln1_g[l], ln1_b[l], w_up[l], ffn_dw_w[l], ffn_dw_b[l], w_down[l], ln2_g[l], ln2_b[l],
                   seq_tile)
    return x
```

```python
import functools
import math

import jax
import jax.numpy as jnp
from jax import lax
from jax.experimental import pallas as pl
from jax.experimental.pallas import tpu as pltpu

HEAD_DIM = 64
N_KV_HEADS = 2
GQA_GROUP = 4
N_HEADS = N_KV_HEADS * GQA_GROUP
WINDOW = 128
CONV_WIDTH = 31
N_BUCKETS = 32
MAX_DISTANCE = 128
FFN_CONV_WIDTH = 3
LN_EPS = 1e-5

SEQ_TILE = 512
CONV_HALO = 32
CONV_ROWS = 32
FFN_CHUNK = 256
LANES = 128
W_GLU, W_SHIFT, W_KV, W_Q, W_ATTN, W_CONV = 1300, 900, 500, 400, 700, 1400
STAGE_ROWS, STAGE_COLS = 256, 1792
VMEM_LIMIT_BYTES = 64 * 1024 * 1024

BF16 = jnp.bfloat16
F32 = jnp.float32


def _layer_norm(r, g, b):
    mu = jnp.mean(r, axis=-1, keepdims=True)
    d = r - mu
    var = jnp.mean(d * d, axis=-1, keepdims=True)
    return d * lax.rsqrt(var + LN_EPS) * g + b


def _rms_norm(y, g):
    return y * lax.rsqrt(jnp.mean(y * y, axis=-1, keepdims=True) + LN_EPS) * g


def _dot_nt(a, b):
    return lax.dot_general(a, b, (((1,), (1,)), ((), ())), preferred_element_type=F32)


def _layer_kernel(alpha, n_tiles, tiles_per_row,
                  x_ref, w_in_hbm, b_in_ref, sinks_ref, tbl_ref, cw_ref, cb_ref, clg_ref, clb_ref,
                  ag_ref, cg_ref, w_out_hbm, b_out_ref, l1g_ref, l1b_ref,
                  w_up_hbm, fw_ref, fb_ref, wd_hbm, l2g_ref, l2b_ref,
                  o_ref,
                  w_in_ref, w_out_ref, w_up_ref, wd_ref, stage_ref, stage_sem, bias_ref, cwb_ref, kv_ref, hbuf_ref, hshift_ref, y_ref, x1_ref, perm_ref, nat_ref,
                  act_ref, cg_carry_ref, cu_carry_ref):
    ts = x_ref.shape[0]
    d_model = x_ref.shape[1]
    d_attn = N_HEADS * HEAD_DIM
    d_kv = N_KV_HEADS * HEAD_DIM
    d_conv = cw_ref.shape[1]
    d_ff = wd_ref.shape[0]
    n_chunks = d_ff // FFN_CHUNK
    n_slabs = d_model // LANES
    g = pl.program_id(0)
    i = jnp.minimum(g, n_tiles - 1) % tiles_per_row
    fi = jnp.maximum(g - 1, 0) % tiles_per_row

    qi = lax.broadcasted_iota(jnp.int32, (WINDOW, WINDOW), 0)
    kj = lax.broadcasted_iota(jnp.int32, (WINDOW, WINDOW), 1)
    tri = kj <= qi

    def weight_blocks(pairs):
        stage_rows, stage_cols = stage_ref.shape[1:]
        blocks = []
        for src_hbm, dst_ref in pairs:
            rows, cols = src_hbm.shape
            width = max(w for w in range(LANES, stage_cols + 1, LANES) if cols % w == 0)
            blocks += [(src_hbm, dst_ref, r0, c0, width)
                       for c0 in range(0, cols, width) for r0 in range(0, rows, stage_rows)]

        def copy(k):
            src_hbm, _, r0, c0, width = blocks[k]
            return pltpu.make_async_copy(src_hbm.at[pl.ds(r0, stage_rows), pl.ds(c0, width)],
                                         stage_ref.at[k % 2, :, pl.ds(0, width)], stage_sem.at[k % 2])

        copy(0).start()
        for k, (_, dst_ref, r0, c0, width) in enumerate(blocks):
            if k + 1 < len(blocks):
                copy(k + 1).start()
            copy(k).wait()
            dst_ref[r0:r0 + stage_rows, c0:c0 + width] = stage_ref[k % 2, :, 0:width].astype(BF16)
            yield

    @pl.when(i == 0)
    def _reset_mixer_history():
        kv_ref[:, 0:WINDOW, :] = jnp.zeros((8, WINDOW, d_kv), BF16)
        hbuf_ref[0:CONV_HALO, :] = jnp.zeros((CONV_HALO, d_conv), F32)

    @pl.when(fi == 0)
    def _reset_ffn_history():
        cg_carry_ref[...] = jnp.zeros(cg_carry_ref.shape, F32)
        cu_carry_ref[...] = jnp.zeros(cu_carry_ref.shape, F32)

    def mixer():
        x = x_ref[...]
        proj = jnp.dot(x.astype(BF16), w_in_ref[...], preferred_element_type=F32) + b_in_ref[...]
        q_end, k_end, v_end, a_end = d_attn, d_attn + d_kv, d_attn + 2 * d_kv, d_attn + 2 * d_kv + d_conv
        yield 0

        hbuf_ref[CONV_HALO:CONV_HALO + ts, :] = proj[:, v_end:a_end] * jax.nn.sigmoid(proj[:, a_end:])
        yield W_GLU
        for s in range(1, 8):
            hshift_ref[s - 1] = hbuf_ref[s:s + hshift_ref.shape[1], :]
            yield W_SHIFT

        lane = lax.broadcasted_iota(jnp.int32, (ts, d_kv), 1)
        lo = lane < HEAD_DIM
        for base, t in ((0, proj[:, q_end:k_end]), (4, proj[:, k_end:v_end])):
            tr = pltpu.roll(t, HEAD_DIM, axis=1)
            zero = jnp.zeros_like(t)
            kv_ref[base + 0, WINDOW:WINDOW + ts, :] = jnp.where(lo, t, zero).astype(BF16)
            kv_ref[base + 1, WINDOW:WINDOW + ts, :] = jnp.where(lo, zero, tr).astype(BF16)
            kv_ref[base + 2, WINDOW:WINDOW + ts, :] = jnp.where(lo, tr, zero).astype(BF16)
            kv_ref[base + 3, WINDOW:WINDOW + ts, :] = jnp.where(lo, zero, t).astype(BF16)
            yield W_KV
        qs = (proj[:, 0:q_end] * (HEAD_DIM ** -0.5)).astype(BF16)
        yield W_Q

        conv, attn = conv_pieces(), attention_pieces(qs)
        for w in conv:
            yield w
            yield next(conv, 0)
            yield next(attn, 0)
        for w in attn:
            yield w

        mix = jnp.dot(y_ref[...], w_out_ref[...], preferred_element_type=F32) + b_out_ref[...]
        x1_ref[...] = _layer_norm(alpha * x + mix, l1g_ref[...], l1b_ref[...])
        yield 0

    def attention_pieces(qs):
        lo_w = lax.broadcasted_iota(jnp.int32, (WINDOW, 2 * HEAD_DIM), 1) < HEAD_DIM
        for j in range(ts // WINDOW):
            r0 = j * WINDOW
            has_prev = (i > 0) | (j > 0)
            prev_mask = jnp.where(has_prev, 0.0, -jnp.inf).astype(F32)
            qb = qs[r0:r0 + WINDOW, :]
            pair_out = []
            for h in range(N_KV_HEADS):
                q2 = jnp.concatenate([qb[:, (2 * h) * LANES:(2 * h + 1) * LANES],
                                      qb[:, (2 * h + 1) * LANES:(2 * h + 2) * LANES]], axis=0)
                probs = {}
                inv = {}
                for ab in range(2):
                    keys = kv_ref[2 * h + ab, r0:r0 + 2 * WINDOW, :]
                    s = _dot_nt(q2, keys)
                    for pp in range(2):
                        hd = 4 * h + 2 * pp + ab
                        s_prev = s[pp * WINDOW:(pp + 1) * WINDOW, 0:WINDOW]
                        s_cur = s[pp * WINDOW:(pp + 1) * WINDOW, WINDOW:2 * WINDOW]
                        sc = jnp.where(tri, s_cur, s_prev + prev_mask) + bias_ref[hd]
                        sink = sinks_ref[hd]
                        m = jnp.maximum(jnp.max(sc, axis=-1, keepdims=True), sink)
                        p = jnp.exp(sc - m)
                        denom = jnp.sum(p, axis=-1, keepdims=True) + jnp.exp(sink - m)
                        zero = jnp.zeros_like(p)
                        probs[(pp, ab)] = jnp.concatenate(
                            [jnp.where(tri, zero, p), jnp.where(tri, p, zero)], axis=1).astype(BF16)
                        inv[(pp, ab)] = 1.0 / denom
                for pp in range(2):
                    va = kv_ref[4 + 2 * h + 0, r0:r0 + 2 * WINDOW, :]
                    vb = kv_ref[4 + 2 * h + 1, r0:r0 + 2 * WINDOW, :]
                    o = (jnp.dot(probs[(pp, 0)], va, preferred_element_type=F32)
                         + jnp.dot(probs[(pp, 1)], vb, preferred_element_type=F32))
                    pair_out.append(o * jnp.where(lo_w, inv[(pp, 0)], inv[(pp, 1)]))
                if h + 1 < N_KV_HEADS:
                    yield W_ATTN
            ya = jnp.concatenate(pair_out, axis=1)
            y_ref[r0:r0 + WINDOW, 0:d_attn] = _rms_norm(ya, ag_ref[...]).astype(BF16)
            yield W_ATTN
        kv_ref[:, 0:WINDOW, :] = kv_ref[:, ts:ts + WINDOW, :]

    def conv_pieces():
        for c in range(ts // CONV_ROWS):
            r0 = c * CONV_ROWS
            acc = jnp.broadcast_to(cb_ref[...], (CONV_ROWS, d_conv))
            for k in range(CONV_WIDTH):
                off = CONV_HALO - (CONV_WIDTH - 1) + k
                s, aligned = off % 8, off - off % 8
                rows = slice(r0 + aligned, r0 + aligned + CONV_ROWS)
                tap = hbuf_ref[rows, :] if s == 0 else hshift_ref[s - 1, rows, :]
                acc = acc + jnp.concatenate([cwb_ref[k]] * (CONV_ROWS // 8), axis=0) * tap
            hn = _layer_norm(acc, clg_ref[...], clb_ref[...])
            y_ref[r0:r0 + CONV_ROWS, d_attn:d_attn + d_conv] = (
                _rms_norm(hn * jax.nn.sigmoid(hn), cg_ref[...]).astype(BF16))
            yield W_CONV
        hbuf_ref[0:CONV_HALO, :] = hbuf_ref[ts:ts + CONV_HALO, :]

    @pl.when(g == 0)
    def _first_step():
        loads = weight_blocks(((w_in_hbm, w_in_ref), (w_out_hbm, w_out_ref), (w_up_hbm, w_up_ref),
                               (wd_hbm, wd_ref)))
        for _ in range(w_in_hbm.shape[0] // stage_ref.shape[1]):
            next(loads)
        for k in range(CONV_WIDTH):
            cwb_ref[k] = jnp.broadcast_to(cw_ref[k:k + 1, :], (8, d_conv))
        n = (qi - kj) & (WINDOW - 1)
        max_exact = N_BUCKETS // 2
        nf = jnp.maximum(n, max_exact).astype(F32)
        large = max_exact + (jnp.log(nf / max_exact) / math.log(MAX_DISTANCE / max_exact)
                             * (N_BUCKETS - max_exact)).astype(jnp.int32)
        large = jnp.minimum(large, N_BUCKETS - 1)
        bucket = jnp.where(n < max_exact, n, large)
        for hd in range(N_HEADS):
            bias = jnp.zeros((WINDOW, WINDOW), F32)
            for bkt in range(N_BUCKETS):
                bias = jnp.where(bucket == bkt, tbl_ref[bkt * N_HEADS + hd], bias)
            bias_ref[hd] = bias
        for _ in mixer():
            next(loads, None)
        for _ in loads:
            pass

    @pl.when(g > 0)
    def _steady_step():
        mixer_pieces = mixer()
        interleavable = (W_GLU + 7 * W_SHIFT + 2 * W_KV + W_Q + (ts // WINDOW) * N_KV_HEADS * W_ATTN
                         + (ts // CONV_ROWS) * W_CONV)

        emitted = [0]

        def emit_mixer(target):
            while emitted[0] < target:
                emitted[0] += next(mixer_pieces)

        seg = ts // 8
        x1_prev = x1_ref[...]
        for l in range(n_slabs):
            for s in range(8):
                for u in range(seg // 8):
                    n0 = s * seg + 8 * u
                    perm_ref[l, pl.ds(seg * u + s, 8, stride=8), :] = x1_prev[n0:n0 + 8, l * LANES:(l + 1) * LANES]
        x1pb = jnp.concatenate([perm_ref[l] for l in range(n_slabs)], axis=1).astype(BF16)
        next(mixer_pieces)

        first_sublane = lax.broadcasted_iota(jnp.int32, (8, FFN_CHUNK), 0) == 0

        def causal_taps(up, carry_ref, c):
            prev = carry_ref[c]
            carry_ref[c] = up[ts - 16:ts]
            roll1 = lambda t: pltpu.roll(t, 1, axis=0)
            head1 = jnp.where(first_sublane, roll1(prev[8:16]), roll1(up[ts - 8:ts]))
            head2 = jnp.where(first_sublane, roll1(prev[0:8]), roll1(up[ts - 16:ts - 8]))
            return (jnp.concatenate([head1, up[0:ts - 8]], axis=0),
                    jnp.concatenate([head2, head1, up[0:ts - 16]], axis=0))

        def neg_conv3(col0, carry_ref, c):
            cols = slice(col0 + c * FFN_CHUNK, col0 + (c + 1) * FFN_CHUNK)
            up = jnp.dot(x1pb, w_up_ref[:, cols], preferred_element_type=F32)
            up1, up2 = causal_taps(up, carry_ref, c)
            w0, w1, w2 = -fw_ref[0:1, cols], -fw_ref[1:2, cols], -fw_ref[2:3, cols]
            return w2 * up + (w1 * up1 + (w0 * up2 - fb_ref[:, cols]))

        for c in range(n_chunks):
            ng = neg_conv3(0, cg_carry_ref, c)
            nu = neg_conv3(d_ff, cu_carry_ref, c)
            act = ng * nu * (1.0 / (1.0 + jnp.exp(ng)))
            act_ref[:, c * FFN_CHUNK:(c + 1) * FFN_CHUNK] = act.astype(BF16)
            emit_mixer(interleavable * (c + 1) // n_chunks)
        ffn = jnp.dot(act_ref[...], wd_ref[...], preferred_element_type=F32)
        for _ in mixer_pieces:
            pass

        x1p = jnp.concatenate([perm_ref[l] for l in range(n_slabs)], axis=1)
        out_p = _layer_norm(alpha * x1p + ffn, l2g_ref[...], l2b_ref[...])

        pitch = nat_ref.shape[1] // 8
        for l in range(n_slabs):
            for v in range(seg):
                nat_ref[l, pl.ds(v, 8, stride=pitch), :] = out_p[v * 8:(v + 1) * 8, l * LANES:(l + 1) * LANES]
        for l in range(n_slabs):
            for s in range(8):
                o_ref[s * seg:(s + 1) * seg, l * LANES:(l + 1) * LANES] = nat_ref[l, s * pitch:s * pitch + seg, :]


def _full_spec(shape):
    return pl.BlockSpec(shape, lambda g: (0,) * len(shape))


def _layer(x, alpha, w_in, b_in, sinks, tbl, cw, cb, clg, clb, ag, cg, w_out, b_out, l1g, l1b,
           w_up, fw, fb, w_down, l2g, l2b, seq_tile):
    bsz, seq, d_model = x.shape
    d_conv = cw.shape[1]
    d_ff = w_down.shape[0]
    d_kv = N_KV_HEADS * HEAD_DIM
    n_chunks = d_ff // FFN_CHUNK
    tiles_per_row = seq // seq_tile
    n_tiles = bsz * tiles_per_row
    seg = seq_tile // 8
    nat_pitch = seg + 8
    assert seq % seq_tile == 0 and seq_tile % WINDOW == 0 and d_ff % FFN_CHUNK == 0
    assert seg % 16 == 0 and d_model % LANES == 0

    row = lambda v: v.reshape(1, -1).astype(F32)
    weights = [w.astype(F32) for w in (w_in, w_out, w_up, w_down)]
    assert all(w.shape[0] % STAGE_ROWS == 0 and w.shape[1] % LANES == 0 for w in weights)
    args = [
        x,
        weights[0], row(b_in), sinks.astype(F32), tbl.reshape(-1).astype(F32),
        cw.astype(F32), row(cb), row(clg), row(clb), row(ag), row(cg),
        weights[1], row(b_out), row(l1g), row(l1b),
        weights[2], fw.astype(F32), row(fb), weights[3], row(l2g), row(l2b),
    ]
    smem = pl.BlockSpec(memory_space=pltpu.SMEM)
    in_hbm = pl.BlockSpec(memory_space=pl.ANY)
    big = {1, 11, 15, 18}

    def mixer_tile(g):
        t = jnp.minimum(g, n_tiles - 1)
        return (t // tiles_per_row, t % tiles_per_row, 0)

    def ffn_tile(g):
        t = jnp.maximum(g - 1, 0)
        return (t // tiles_per_row, t % tiles_per_row, 0)

    in_specs = [pl.BlockSpec((None, seq_tile, d_model), mixer_tile)]
    for idx, a in enumerate(args[1:], start=1):
        in_specs.append(smem if idx in (3, 4) else in_hbm if idx in big else _full_spec(a.shape))

    scratch = [pltpu.VMEM(w.shape, BF16) for w in weights] + [
        pltpu.VMEM((2, STAGE_ROWS, STAGE_COLS), F32),
        pltpu.SemaphoreType.DMA((2,)),
        pltpu.VMEM((N_HEADS, WINDOW, WINDOW), F32),
        pltpu.VMEM((CONV_WIDTH, 8, d_conv), F32),
        pltpu.VMEM((8, seq_tile + WINDOW, d_kv), BF16),
        pltpu.VMEM((seq_tile + CONV_HALO, d_conv), F32),
        pltpu.VMEM((7, seq_tile + CONV_HALO - 8, d_conv), F32),
        pltpu.VMEM((seq_tile, d_model), BF16),
        pltpu.VMEM((seq_tile, d_model), F32),
        pltpu.VMEM((d_model // LANES, seq_tile, LANES), F32),
        pltpu.VMEM((d_model // LANES, 8 * nat_pitch, LANES), F32),
        pltpu.VMEM((seq_tile, d_ff), BF16),
        pltpu.VMEM((n_chunks, 16, FFN_CHUNK), F32),
        pltpu.VMEM((n_chunks, 16, FFN_CHUNK), F32),
    ]
    return pl.pallas_call(
        functools.partial(_layer_kernel, alpha, n_tiles, tiles_per_row),
        grid=(n_tiles + 1,),
        in_specs=in_specs,
        out_specs=pl.BlockSpec((None, seq_tile, d_model), ffn_tile),
        out_shape=jax.ShapeDtypeStruct(x.shape, x.dtype),
        scratch_shapes=scratch,
        compiler_params=pltpu.CompilerParams(
            dimension_semantics=("arbitrary",),
            vmem_limit_bytes=VMEM_LIMIT_BYTES),
        name="hybrid_layer",
    )(*args)


def kernel(x, w_in, b_in, attn_sinks, rel_bias_table, conv_dw_w, conv_dw_b, conv_ln_g, conv_ln_b,
           attn_out_gain, conv_out_gain, w_out, b_out, ln1_g, ln1_b, w_up, ffn_dw_w, ffn_dw_b, w_down,
           ln2_g, ln2_b, seq_tile=SEQ_TILE):
    depth = w_in.shape[0]
    alpha = (2.0 * depth) ** 0.25
    for l in range(depth):
        x = _layer(x, alpha, w_in[l], b_in[l], attn_sinks[l], rel_bias_table, conv_dw_w[l], conv_dw_b[l],
                   conv_ln_g[l], conv_ln_b[l], attn_out_gain[l], conv_out_gain[l], w_out[l], b_out[l],
                   ln1_g[l], ln1_b[l], w_up[l], ffn_dw_w[l], ffn_dw_b[l], w_down[l], ln2_g[l], ln2_b[l],
                   seq_tile)
    return x
```

```python
import functools
import math

import jax
import jax.numpy as jnp
from jax import lax
from jax.experimental import pallas as pl
from jax.experimental.pallas import tpu as pltpu

HEAD_DIM = 64
N_KV_HEADS = 2
GQA_GROUP = 4
N_HEADS = N_KV_HEADS * GQA_GROUP
WINDOW = 128
CONV_WIDTH = 31
N_BUCKETS = 32
MAX_DISTANCE = 128
FFN_CONV_WIDTH = 3
LN_EPS = 1e-5

SEQ_TILE = 512
CONV_HALO = 32
CONV_ROWS = 32
FFN_CHUNK = 256
LANES = 128
W_GLU, W_SHIFT, W_KV, W_Q, W_ATTN, W_CONV = 1300, 900, 500, 400, 700, 1400
STAGE_ROWS, STAGE_COLS = 256, 1792
STAGE_SLOTS = 3
VMEM_LIMIT_BYTES = 58 * 1024 * 1024

BF16 = jnp.bfloat16
F32 = jnp.float32


def _layer_norm(r, g, b):
    mu = jnp.mean(r, axis=-1, keepdims=True)
    d = r - mu
    var = jnp.mean(d * d, axis=-1, keepdims=True)
    return d * lax.rsqrt(var + LN_EPS) * g + b


def _rms_norm(y, g):
    return y * lax.rsqrt(jnp.mean(y * y, axis=-1, keepdims=True) + LN_EPS) * g


def _dot_nt(a, b):
    return lax.dot_general(a, b, (((1,), (1,)), ((), ())), preferred_element_type=F32)


def _layer_kernel(alpha, n_tiles, tiles_per_row,
                  x_ref, w_in_hbm, b_in_ref, sinks_ref, tbl_ref, cw_ref, cb_ref, clg_ref, clb_ref,
                  ag_ref, cg_ref, w_out_hbm, b_out_ref, l1g_ref, l1b_ref,
                  w_up_hbm, fw_ref, fb_ref, wd_hbm, l2g_ref, l2b_ref,
                  o_ref,
                  w_in_ref, w_out_ref, w_up_ref, wd_ref, stage_ref, stage_sem, bias_ref, cwb_ref, kv_ref, hbuf_ref, hshift_ref, y_ref, x1_ref, perm_ref, nat_ref,
                  act_ref, cg_carry_ref, cu_carry_ref):
    ts = x_ref.shape[0]
    d_model = x_ref.shape[1]
    d_attn = N_HEADS * HEAD_DIM
    d_kv = N_KV_HEADS * HEAD_DIM
    d_conv = cw_ref.shape[1]
    d_ff = wd_ref.shape[0]
    n_chunks = d_ff // FFN_CHUNK
    n_slabs = d_model // LANES
    g = pl.program_id(0)
    i = jnp.minimum(g, n_tiles - 1) % tiles_per_row
    fi = jnp.maximum(g - 1, 0) % tiles_per_row

    qi = lax.broadcasted_iota(jnp.int32, (WINDOW, WINDOW), 0)
    kj = lax.broadcasted_iota(jnp.int32, (WINDOW, WINDOW), 1)
    tri = kj <= qi

    def load_weights(pairs):
        n_slots, stage_rows, stage_cols = stage_ref.shape
        blocks = []
        for src_hbm, dst_ref in pairs:
            rows, cols = src_hbm.shape
            width = max(w for w in range(LANES, stage_cols + 1, LANES) if cols % w == 0)
            blocks += [(src_hbm, dst_ref, r0, c0, width)
                       for c0 in range(0, cols, width) for r0 in range(0, rows, stage_rows)]

        def copy(k):
            src_hbm, _, r0, c0, width = blocks[k]
            slot = k % n_slots
            return pltpu.make_async_copy(src_hbm.at[pl.ds(r0, stage_rows), pl.ds(c0, width)],
                                         stage_ref.at[slot, :, pl.ds(0, width)], stage_sem.at[slot])

        for k in range(min(n_slots - 1, len(blocks))):
            copy(k).start()
        for k, (_, dst_ref, r0, c0, width) in enumerate(blocks):
            if k + n_slots - 1 < len(blocks):
                copy(k + n_slots - 1).start()
            copy(k).wait()
            dst_ref[r0:r0 + stage_rows, c0:c0 + width] = stage_ref[k % n_slots, :, 0:width].astype(BF16)

    @pl.when(g == 0)
    def _first_step():
        load_weights(((w_in_hbm, w_in_ref), (w_out_hbm, w_out_ref), (w_up_hbm, w_up_ref), (wd_hbm, wd_ref)))
        x1_ref[...] = jnp.zeros(x1_ref.shape, F32)
        for k in range(CONV_WIDTH):
            cwb_ref[k] = jnp.broadcast_to(cw_ref[k:k + 1, :], (8, d_conv))
        n = (qi - kj) & (WINDOW - 1)
        max_exact = N_BUCKETS // 2
        nf = jnp.maximum(n, max_exact).astype(F32)
        large = max_exact + (jnp.log(nf / max_exact) / math.log(MAX_DISTANCE / max_exact)
                             * (N_BUCKETS - max_exact)).astype(jnp.int32)
        large = jnp.minimum(large, N_BUCKETS - 1)
        bucket = jnp.where(n < max_exact, n, large)
        for hd in range(N_HEADS):
            bias = jnp.zeros((WINDOW, WINDOW), F32)
            for bkt in range(N_BUCKETS):
                bias = jnp.where(bucket == bkt, tbl_ref[bkt * N_HEADS + hd], bias)
            bias_ref[hd] = bias

    @pl.when(i == 0)
    def _reset_mixer_history():
        kv_ref[:, 0:WINDOW, :] = jnp.zeros((8, WINDOW, d_kv), BF16)
        hbuf_ref[0:CONV_HALO, :] = jnp.zeros((CONV_HALO, d_conv), F32)

    @pl.when(fi == 0)
    def _reset_ffn_history():
        cg_carry_ref[...] = jnp.zeros(cg_carry_ref.shape, F32)
        cu_carry_ref[...] = jnp.zeros(cu_carry_ref.shape, F32)

    def mixer():
        x = x_ref[...]
        proj = jnp.dot(x.astype(BF16), w_in_ref[...], preferred_element_type=F32) + b_in_ref[...]
        q_end, k_end, v_end, a_end = d_attn, d_attn + d_kv, d_attn + 2 * d_kv, d_attn + 2 * d_kv + d_conv
        yield 0

        hbuf_ref[CONV_HALO:CONV_HALO + ts, :] = proj[:, v_end:a_end] * jax.nn.sigmoid(proj[:, a_end:])
        yield W_GLU
        for s in range(1, 8):
            hshift_ref[s - 1] = hbuf_ref[s:s + hshift_ref.shape[1], :]
            yield W_SHIFT

        lane = lax.broadcasted_iota(jnp.int32, (ts, d_kv), 1)
        lo = lane < HEAD_DIM
        for base, t in ((0, proj[:, q_end:k_end]), (4, proj[:, k_end:v_end])):
            tr = pltpu.roll(t, HEAD_DIM, axis=1)
            zero = jnp.zeros_like(t)
            kv_ref[base + 0, WINDOW:WINDOW + ts, :] = jnp.where(lo, t, zero).astype(BF16)
            kv_ref[base + 1, WINDOW:WINDOW + ts, :] = jnp.where(lo, zero, tr).astype(BF16)
            kv_ref[base + 2, WINDOW:WINDOW + ts, :] = jnp.where(lo, tr, zero).astype(BF16)
            kv_ref[base + 3, WINDOW:WINDOW + ts, :] = jnp.where(lo, zero, t).astype(BF16)
            yield W_KV
        qs = (proj[:, 0:q_end] * (HEAD_DIM ** -0.5)).astype(BF16)
        yield W_Q

        conv, attn = conv_pieces(), attention_pieces(qs)
        for w in conv:
            yield w
            yield next(conv, 0)
            yield next(attn, 0)
        for w in attn:
            yield w

        mix = jnp.dot(y_ref[...], w_out_ref[...], preferred_element_type=F32) + b_out_ref[...]
        x1_ref[...] = _layer_norm(alpha * x + mix, l1g_ref[...], l1b_ref[...])
        yield 0

    def attention_pieces(qs):
        lo_w = lax.broadcasted_iota(jnp.int32, (WINDOW, 2 * HEAD_DIM), 1) < HEAD_DIM
        for j in range(ts // WINDOW):
            r0 = j * WINDOW
            has_prev = (i > 0) | (j > 0)
            prev_mask = jnp.where(has_prev, 0.0, -jnp.inf).astype(F32)
            qb = qs[r0:r0 + WINDOW, :]
            pair_out = []
            for h in range(N_KV_HEADS):
                q2 = jnp.concatenate([qb[:, (2 * h) * LANES:(2 * h + 1) * LANES],
                                      qb[:, (2 * h + 1) * LANES:(2 * h + 2) * LANES]], axis=0)
                probs = {}
                inv = {}
                for ab in range(2):
                    keys = kv_ref[2 * h + ab, r0:r0 + 2 * WINDOW, :]
                    s = _dot_nt(q2, keys)
                    for pp in range(2):
                        hd = 4 * h + 2 * pp + ab
                        s_prev = s[pp * WINDOW:(pp + 1) * WINDOW, 0:WINDOW]
                        s_cur = s[pp * WINDOW:(pp + 1) * WINDOW, WINDOW:2 * WINDOW]
                        sc = jnp.where(tri, s_cur, s_prev + prev_mask) + bias_ref[hd]
                        sink = sinks_ref[hd]
                        m = jnp.maximum(jnp.max(sc, axis=-1, keepdims=True), sink)
                        p = jnp.exp(sc - m)
                        denom = jnp.sum(p, axis=-1, keepdims=True) + jnp.exp(sink - m)
                        zero = jnp.zeros_like(p)
                        probs[(pp, ab)] = jnp.concatenate(
                            [jnp.where(tri, zero, p), jnp.where(tri, p, zero)], axis=1).astype(BF16)
                        inv[(pp, ab)] = 1.0 / denom
                for pp in range(2):
                    va = kv_ref[4 + 2 * h + 0, r0:r0 + 2 * WINDOW, :]
                    vb = kv_ref[4 + 2 * h + 1, r0:r0 + 2 * WINDOW, :]
                    o = (jnp.dot(probs[(pp, 0)], va, preferred_element_type=F32)
                         + jnp.dot(probs[(pp, 1)], vb, preferred_element_type=F32))
                    pair_out.append(o * jnp.where(lo_w, inv[(pp, 0)], inv[(pp, 1)]))
                if h + 1 < N_KV_HEADS:
                    yield W_ATTN
            ya = jnp.concatenate(pair_out, axis=1)
            y_ref[r0:r0 + WINDOW, 0:d_attn] = _rms_norm(ya, ag_ref[...]).astype(BF16)
            yield W_ATTN
        kv_ref[:, 0:WINDOW, :] = kv_ref[:, ts:ts + WINDOW, :]

    def conv_pieces():
        for c in range(ts // CONV_ROWS):
            r0 = c * CONV_ROWS
            acc = jnp.broadcast_to(cb_ref[...], (CONV_ROWS, d_conv))
            for k in range(CONV_WIDTH):
                off = CONV_HALO - (CONV_WIDTH - 1) + k
                s, aligned = off % 8, off - off % 8
                rows = slice(r0 + aligned, r0 + aligned + CONV_ROWS)
                tap = hbuf_ref[rows, :] if s == 0 else hshift_ref[s - 1, rows, :]
                acc = acc + jnp.concatenate([cwb_ref[k]] * (CONV_ROWS // 8), axis=0) * tap
            hn = _layer_norm(acc, clg_ref[...], clb_ref[...])
            y_ref[r0:r0 + CONV_ROWS, d_attn:d_attn + d_conv] = (
                _rms_norm(hn * jax.nn.sigmoid(hn), cg_ref[...]).astype(BF16))
            yield W_CONV
        hbuf_ref[0:CONV_HALO, :] = hbuf_ref[ts:ts + CONV_HALO, :]

    mixer_pieces = mixer()
    interleavable = (W_GLU + 7 * W_SHIFT + 2 * W_KV + W_Q + (ts // WINDOW) * N_KV_HEADS * W_ATTN
                     + (ts // CONV_ROWS) * W_CONV)

    emitted = [0]

    def emit_mixer(target):
        while emitted[0] < target:
            emitted[0] += next(mixer_pieces)

    seg = ts // 8
    x1_prev = x1_ref[...]
    for l in range(n_slabs):
        for s in range(8):
            for u in range(seg // 8):
                n0 = s * seg + 8 * u
                perm_ref[l, pl.ds(seg * u + s, 8, stride=8), :] = x1_prev[n0:n0 + 8, l * LANES:(l + 1) * LANES]
    x1pb = jnp.concatenate([perm_ref[l] for l in range(n_slabs)], axis=1).astype(BF16)
    next(mixer_pieces)

    first_sublane = lax.broadcasted_iota(jnp.int32, (8, FFN_CHUNK), 0) == 0

    def causal_taps(up, carry_ref, c):
        prev = carry_ref[c]
        carry_ref[c] = up[ts - 16:ts]
        roll1 = lambda t: pltpu.roll(t, 1, axis=0)
        head1 = jnp.where(first_sublane, roll1(prev[8:16]), roll1(up[ts - 8:ts]))
        head2 = jnp.where(first_sublane, roll1(prev[0:8]), roll1(up[ts - 16:ts - 8]))
        return (jnp.concatenate([head1, up[0:ts - 8]], axis=0),
                jnp.concatenate([head2, head1, up[0:ts - 16]], axis=0))

    def neg_conv3(col0, carry_ref, c):
        cols = slice(col0 + c * FFN_CHUNK, col0 + (c + 1) * FFN_CHUNK)
        up = jnp.dot(x1pb, w_up_ref[:, cols], preferred_element_type=F32)
        up1, up2 = causal_taps(up, carry_ref, c)
        w0, w1, w2 = -fw_ref[0:1, cols], -fw_ref[1:2, cols], -fw_ref[2:3, cols]
        return w2 * up + (w1 * up1 + (w0 * up2 - fb_ref[:, cols]))

    for c in range(n_chunks):
        ng = neg_conv3(0, cg_carry_ref, c)
        nu = neg_conv3(d_ff, cu_carry_ref, c)
        act = ng * nu * (1.0 / (1.0 + jnp.exp(ng)))
        act_ref[:, c * FFN_CHUNK:(c + 1) * FFN_CHUNK] = act.astype(BF16)
        emit_mixer(interleavable * (c + 1) // n_chunks)
    ffn = jnp.dot(act_ref[...], wd_ref[...], preferred_element_type=F32)
    for _ in mixer_pieces:
        pass

    x1p = jnp.concatenate([perm_ref[l] for l in range(n_slabs)], axis=1)
    out_p = _layer_norm(alpha * x1p + ffn, l2g_ref[...], l2b_ref[...])

    pitch = nat_ref.shape[1] // 8
    for l in range(n_slabs):
        for v in range(seg):
            nat_ref[l, pl.ds(v, 8, stride=pitch), :] = out_p[v * 8:(v + 1) * 8, l * LANES:(l + 1) * LANES]
    for l in range(n_slabs):
        for s in range(8):
            o_ref[s * seg:(s + 1) * seg, l * LANES:(l + 1) * LANES] = nat_ref[l, s * pitch:s * pitch + seg, :]


def _full_spec(shape):
    return pl.BlockSpec(shape, lambda g: (0,) * len(shape))


def _layer(x, alpha, w_in, b_in, sinks, tbl, cw, cb, clg, clb, ag, cg, w_out, b_out, l1g, l1b,
           w_up, fw, fb, w_down, l2g, l2b, seq_tile):
    bsz, seq, d_model = x.shape
    d_conv = cw.shape[1]
    d_ff = w_down.shape[0]
    d_kv = N_KV_HEADS * HEAD_DIM
    n_chunks = d_ff // FFN_CHUNK
    tiles_per_row = seq // seq_tile
    n_tiles = bsz * tiles_per_row
    seg = seq_tile // 8
    nat_pitch = seg + 8
    assert seq % seq_tile == 0 and seq_tile % WINDOW == 0 and d_ff % FFN_CHUNK == 0
    assert seg % 16 == 0 and d_model % LANES == 0

    row = lambda v: v.reshape(1, -1).astype(F32)
    weights = [w.astype(F32) for w in (w_in, w_out, w_up, w_down)]
    assert all(w.shape[0] % STAGE_ROWS == 0 and w.shape[1] % LANES == 0 for w in weights)
    args = [
        x,
        weights[0], row(b_in), sinks.astype(F32), tbl.reshape(-1).astype(F32),
        cw.astype(F32), row(cb), row(clg), row(clb), row(ag), row(cg),
        weights[1], row(b_out), row(l1g), row(l1b),
        weights[2], fw.astype(F32), row(fb), weights[3], row(l2g), row(l2b),
    ]
    smem = pl.BlockSpec(memory_space=pltpu.SMEM)
    in_hbm = pl.BlockSpec(memory_space=pl.ANY)
    big = {1, 11, 15, 18}

    def mixer_tile(g):
        t = jnp.minimum(g, n_tiles - 1)
        return (t // tiles_per_row, t % tiles_per_row, 0)

    def ffn_tile(g):
        t = jnp.maximum(g - 1, 0)
        return (t // tiles_per_row, t % tiles_per_row, 0)

    in_specs = [pl.BlockSpec((None, seq_tile, d_model), mixer_tile)]
    for idx, a in enumerate(args[1:], start=1):
        in_specs.append(smem if idx in (3, 4) else in_hbm if idx in big else _full_spec(a.shape))

    scratch = [pltpu.VMEM(w.shape, BF16) for w in weights] + [
        pltpu.VMEM((STAGE_SLOTS, STAGE_ROWS, STAGE_COLS), F32),
        pltpu.SemaphoreType.DMA((STAGE_SLOTS,)),
        pltpu.VMEM((N_HEADS, WINDOW, WINDOW), F32),
        pltpu.VMEM((CONV_WIDTH, 8, d_conv), F32),
        pltpu.VMEM((8, seq_tile + WINDOW, d_kv), BF16),
        pltpu.VMEM((seq_tile + CONV_HALO, d_conv), F32),
        pltpu.VMEM((7, seq_tile + CONV_HALO - 8, d_conv), F32),
        pltpu.VMEM((seq_tile, d_model), BF16),
        pltpu.VMEM((seq_tile, d_model), F32),
        pltpu.VMEM((d_model // LANES, seq_tile, LANES), F32),
        pltpu.VMEM((d_model // LANES, 8 * nat_pitch, LANES), F32),
        pltpu.VMEM((seq_tile, d_ff), BF16),
        pltpu.VMEM((n_chunks, 16, FFN_CHUNK), F32),
        pltpu.VMEM((n_chunks, 16, FFN_CHUNK), F32),
    ]
    return pl.pallas_call(
        functools.partial(_layer_kernel, alpha, n_tiles, tiles_per_row),
        grid=(n_tiles + 1,),
        in_specs=in_specs,
        out_specs=pl.BlockSpec((None, seq_tile, d_model), ffn_tile),
        out_shape=jax.ShapeDtypeStruct(x.shape, x.dtype),
        scratch_shapes=scratch,
        compiler_params=pltpu.CompilerParams(
            dimension_semantics=("arbitrary",),
            vmem_limit_bytes=VMEM_LIMIT_BYTES),
        name="hybrid_layer",
    )(*args)


def kernel(x, w_in, b_in, attn_sinks, rel_bias_table, conv_dw_w, conv_dw_b, conv_ln_g, conv_ln_b,
           attn_out_gain, conv_out_gain, w_out, b_out, ln1_g, ln1_b, w_up, ffn_dw_w, ffn_dw_b, w_down,
           ln2_g, ln2_b, seq_tile=SEQ_TILE):
    depth = w_in.shape[0]
    alpha = (2.0 * depth) ** 0.25
    for l in range(depth):
        x = _layer(x, alpha, w_in[l], b_in[l], attn_sinks[l], rel_bias_table, conv_dw_w[l], conv_dw_b[l],
                   conv_ln_g[l], conv_ln_b[l], attn_out_gain[l], conv_out_gain[l], w_out[l], b_out[l],
                   ln1_g[l], ln1_b[l], w_up[l], ffn_dw_w[l], ffn_dw_b[l], w_down[l], ln2_g[l], ln2_b[l],
                   seq_tile)
    return x
```

```python
import functools
import math

import jax
import jax.numpy as jnp
from jax import lax
from jax.experimental import pallas as pl
from jax.experimental.pallas import tpu as pltpu

HEAD_DIM = 64
N_KV_HEADS = 2
GQA_GROUP = 4
N_HEADS = N_KV_HEADS * GQA_GROUP
WINDOW = 128
CONV_WIDTH = 31
N_BUCKETS = 32
MAX_DISTANCE = 128
FFN_CONV_WIDTH = 3
LN_EPS = 1e-5

SEQ_TILE = 512
CONV_HALO = 32
CONV_ROWS = 32
FFN_CHUNK = 256
LANES = 128
W_GLU, W_SHIFT, W_KV, W_Q, W_ATTN, W_CONV = 1300, 900, 500, 400, 350, 1400
STAGE_ROWS, STAGE_COLS = 256, 1792
STAGE_SLOTS = 3
VMEM_LIMIT_BYTES = 58 * 1024 * 1024

BF16 = jnp.bfloat16
F32 = jnp.float32


def _layer_norm(r, g, b):
    mu = jnp.mean(r, axis=-1, keepdims=True)
    d = r - mu
    var = jnp.mean(d * d, axis=-1, keepdims=True)
    return d * lax.rsqrt(var + LN_EPS) * g + b


def _rms_norm(y, g):
    return y * lax.rsqrt(jnp.mean(y * y, axis=-1, keepdims=True) + LN_EPS) * g


def _dot_nt(a, b):
    return lax.dot_general(a, b, (((1,), (1,)), ((), ())), preferred_element_type=F32)


def _layer_kernel(alpha, n_tiles, tiles_per_row,
                  x_ref, w_in_hbm, b_in_ref, sinks_ref, tbl_ref, cw_ref, cb_ref, clg_ref, clb_ref,
                  ag_ref, cg_ref, w_out_hbm, b_out_ref, l1g_ref, l1b_ref,
                  w_up_hbm, fw_ref, fb_ref, wd_hbm, l2g_ref, l2b_ref,
                  o_ref,
                  w_in_ref, w_out_ref, w_up_ref, wd_ref, stage_ref, stage_sem, bias_ref, cwb_ref, kv_ref, hbuf_ref, hshift_ref, y_ref, x1_ref, perm_ref, nat_ref,
                  act_ref, cg_carry_ref, cu_carry_ref):
    ts = x_ref.shape[0]
    d_model = x_ref.shape[1]
    d_attn = N_HEADS * HEAD_DIM
    d_kv = N_KV_HEADS * HEAD_DIM
    d_conv = cw_ref.shape[1]
    d_ff = wd_ref.shape[0]
    n_chunks = d_ff // FFN_CHUNK
    n_slabs = d_model // LANES
    g = pl.program_id(0)
    i = jnp.minimum(g, n_tiles - 1) % tiles_per_row
    fi = jnp.maximum(g - 1, 0) % tiles_per_row

    qi = lax.broadcasted_iota(jnp.int32, (WINDOW, WINDOW), 0)
    kj = lax.broadcasted_iota(jnp.int32, (WINDOW, WINDOW), 1)
    tri = kj <= qi

    def load_weights(pairs):
        n_slots, stage_rows, stage_cols = stage_ref.shape
        blocks = []
        for src_hbm, dst_ref in pairs:
            rows, cols = src_hbm.shape
            width = max(w for w in range(LANES, stage_cols + 1, LANES) if cols % w == 0)
            blocks += [(src_hbm, dst_ref, r0, c0, width)
                       for c0 in range(0, cols, width) for r0 in range(0, rows, stage_rows)]

        def copy(k):
            src_hbm, _, r0, c0, width = blocks[k]
            slot = k % n_slots
            return pltpu.make_async_copy(src_hbm.at[pl.ds(r0, stage_rows), pl.ds(c0, width)],
                                         stage_ref.at[slot, :, pl.ds(0, width)], stage_sem.at[slot])

        for k in range(min(n_slots - 1, len(blocks))):
            copy(k).start()
        for k, (_, dst_ref, r0, c0, width) in enumerate(blocks):
            if k + n_slots - 1 < len(blocks):
                copy(k + n_slots - 1).start()
            copy(k).wait()
            dst_ref[r0:r0 + stage_rows, c0:c0 + width] = stage_ref[k % n_slots, :, 0:width].astype(BF16)

    @pl.when(g == 0)
    def _first_step():
        load_weights(((w_in_hbm, w_in_ref), (w_out_hbm, w_out_ref), (w_up_hbm, w_up_ref), (wd_hbm, wd_ref)))
        x1_ref[...] = jnp.zeros(x1_ref.shape, F32)
        for k in range(CONV_WIDTH):
            cwb_ref[k] = jnp.broadcast_to(cw_ref[k:k + 1, :], (8, d_conv))
        n = (qi - kj) & (WINDOW - 1)
        max_exact = N_BUCKETS // 2
        nf = jnp.maximum(n, max_exact).astype(F32)
        large = max_exact + (jnp.log(nf / max_exact) / math.log(MAX_DISTANCE / max_exact)
                             * (N_BUCKETS - max_exact)).astype(jnp.int32)
        large = jnp.minimum(large, N_BUCKETS - 1)
        bucket = jnp.where(n < max_exact, n, large)
        for hd in range(N_HEADS):
            bias = jnp.zeros((WINDOW, WINDOW), F32)
            for bkt in range(N_BUCKETS):
                bias = jnp.where(bucket == bkt, tbl_ref[bkt * N_HEADS + hd], bias)
            bias_ref[hd] = bias

    @pl.when(i == 0)
    def _reset_mixer_history():
        kv_ref[:, 0:WINDOW, :] = jnp.zeros((8, WINDOW, d_kv), BF16)
        hbuf_ref[0:CONV_HALO, :] = jnp.zeros((CONV_HALO, d_conv), F32)

    @pl.when(fi == 0)
    def _reset_ffn_history():
        cg_carry_ref[...] = jnp.zeros(cg_carry_ref.shape, F32)
        cu_carry_ref[...] = jnp.zeros(cu_carry_ref.shape, F32)

    def mixer():
        x = x_ref[...]
        proj = jnp.dot(x.astype(BF16), w_in_ref[...], preferred_element_type=F32) + b_in_ref[...]
        q_end, k_end, v_end, a_end = d_attn, d_attn + d_kv, d_attn + 2 * d_kv, d_attn + 2 * d_kv + d_conv
        yield 0

        hbuf_ref[CONV_HALO:CONV_HALO + ts, :] = proj[:, v_end:a_end] * jax.nn.sigmoid(proj[:, a_end:])
        yield W_GLU
        for s in range(1, 8):
            hshift_ref[s - 1] = hbuf_ref[s:s + hshift_ref.shape[1], :]
            yield W_SHIFT

        lane = lax.broadcasted_iota(jnp.int32, (ts, d_kv), 1)
        lo = lane < HEAD_DIM
        for base, t in ((0, proj[:, q_end:k_end]), (4, proj[:, k_end:v_end])):
            tr = pltpu.roll(t, HEAD_DIM, axis=1)
            zero = jnp.zeros_like(t)
            kv_ref[base + 0, WINDOW:WINDOW + ts, :] = jnp.where(lo, t, zero).astype(BF16)
            kv_ref[base + 1, WINDOW:WINDOW + ts, :] = jnp.where(lo, zero, tr).astype(BF16)
            kv_ref[base + 2, WINDOW:WINDOW + ts, :] = jnp.where(lo, tr, zero).astype(BF16)
            kv_ref[base + 3, WINDOW:WINDOW + ts, :] = jnp.where(lo, zero, t).astype(BF16)
            yield W_KV
        qs = (proj[:, 0:q_end] * (HEAD_DIM ** -0.5)).astype(BF16)
        yield W_Q

        conv, attn = conv_pieces(), attention_pieces(qs)
        for w in conv:
            yield w
            yield next(conv, 0)
            yield next(attn, 0)
        for w in attn:
            yield w

        mix = jnp.dot(y_ref[...], w_out_ref[...], preferred_element_type=F32) + b_out_ref[...]
        x1_ref[...] = _layer_norm(alpha * x + mix, l1g_ref[...], l1b_ref[...])
        yield 0

    def attention_pieces(qs):
        lo_w = lax.broadcasted_iota(jnp.int32, (WINDOW, 2 * HEAD_DIM), 1) < HEAD_DIM
        for j in range(ts // WINDOW):
            r0 = j * WINDOW
            has_prev = (i > 0) | (j > 0)
            prev_mask = jnp.where(has_prev, 0.0, -jnp.inf).astype(F32)
            qb = qs[r0:r0 + WINDOW, :]
            pair_out = []
            for h in range(N_KV_HEADS):
                q2 = jnp.concatenate([qb[:, (2 * h) * LANES:(2 * h + 1) * LANES],
                                      qb[:, (2 * h + 1) * LANES:(2 * h + 2) * LANES]], axis=0)
                probs = {}
                inv = {}
                for ab in range(2):
                    keys = kv_ref[2 * h + ab, r0:r0 + 2 * WINDOW, :]
                    s = _dot_nt(q2, keys)
                    for pp in range(2):
                        hd = 4 * h + 2 * pp + ab
                        s_prev = s[pp * WINDOW:(pp + 1) * WINDOW, 0:WINDOW]
                        s_cur = s[pp * WINDOW:(pp + 1) * WINDOW, WINDOW:2 * WINDOW]
                        sc = jnp.where(tri, s_cur, s_prev + prev_mask) + bias_ref[hd]
                        sink = sinks_ref[hd]
                        m = jnp.maximum(jnp.max(sc, axis=-1, keepdims=True), sink)
                        p = jnp.exp(sc - m)
                        denom = jnp.sum(p, axis=-1, keepdims=True) + jnp.exp(sink - m)
                        zero = jnp.zeros_like(p)
                        probs[(pp, ab)] = jnp.concatenate(
                            [jnp.where(tri, zero, p), jnp.where(tri, p, zero)], axis=1).astype(BF16)
                        inv[(pp, ab)] = 1.0 / denom
                yield W_ATTN
                for pp in range(2):
                    va = kv_ref[4 + 2 * h + 0, r0:r0 + 2 * WINDOW, :]
                    vb = kv_ref[4 + 2 * h + 1, r0:r0 + 2 * WINDOW, :]
                    o = (jnp.dot(probs[(pp, 0)], va, preferred_element_type=F32)
                         + jnp.dot(probs[(pp, 1)], vb, preferred_element_type=F32))
                    pair_out.append(o * jnp.where(lo_w, inv[(pp, 0)], inv[(pp, 1)]))
                if h + 1 < N_KV_HEADS:
                    yield W_ATTN
            ya = jnp.concatenate(pair_out, axis=1)
            y_ref[r0:r0 + WINDOW, 0:d_attn] = _rms_norm(ya, ag_ref[...]).astype(BF16)
            yield W_ATTN
        kv_ref[:, 0:WINDOW, :] = kv_ref[:, ts:ts + WINDOW, :]

    def conv_pieces():
        for c in range(ts // CONV_ROWS):
            r0 = c * CONV_ROWS
            acc = jnp.broadcast_to(cb_ref[...], (CONV_ROWS, d_conv))
            for k in range(CONV_WIDTH):
                off = CONV_HALO - (CONV_WIDTH - 1) + k
                s, aligned = off % 8, off - off % 8
                rows = slice(r0 + aligned, r0 + aligned + CONV_ROWS)
                tap = hbuf_ref[rows, :] if s == 0 else hshift_ref[s - 1, rows, :]
                acc = acc + jnp.concatenate([cwb_ref[k]] * (CONV_ROWS // 8), axis=0) * tap
            hn = _layer_norm(acc, clg_ref[...], clb_ref[...])
            y_ref[r0:r0 + CONV_ROWS, d_attn:d_attn + d_conv] = (
                _rms_norm(hn * jax.nn.sigmoid(hn), cg_ref[...]).astype(BF16))
            yield W_CONV
        hbuf_ref[0:CONV_HALO, :] = hbuf_ref[ts:ts + CONV_HALO, :]

    mixer_pieces = mixer()
    interleavable = (W_GLU + 7 * W_SHIFT + 2 * W_KV + W_Q + 2 * (ts // WINDOW) * N_KV_HEADS * W_ATTN
                     + (ts // CONV_ROWS) * W_CONV)

    emitted = [0]

    def emit_mixer(target):
        while emitted[0] < target:
            emitted[0] += next(mixer_pieces)

    seg = ts // 8
    x1_prev = x1_ref[...]
    for l in range(n_slabs):
        for s in range(8):
            for u in range(seg // 8):
                n0 = s * seg + 8 * u
                perm_ref[l, pl.ds(seg * u + s, 8, stride=8), :] = x1_prev[n0:n0 + 8, l * LANES:(l + 1) * LANES]
    x1pb = jnp.concatenate([perm_ref[l] for l in range(n_slabs)], axis=1).astype(BF16)
    next(mixer_pieces)

    first_sublane = lax.broadcasted_iota(jnp.int32, (8, FFN_CHUNK), 0) == 0

    def causal_taps(up, carry_ref, c):
        prev = carry_ref[c]
        carry_ref[c] = up[ts - 16:ts]
        roll1 = lambda t: pltpu.roll(t, 1, axis=0)
        head1 = jnp.where(first_sublane, roll1(prev[8:16]), roll1(up[ts - 8:ts]))
        head2 = jnp.where(first_sublane, roll1(prev[0:8]), roll1(up[ts - 16:ts - 8]))
        return (jnp.concatenate([head1, up[0:ts - 8]], axis=0),
                jnp.concatenate([head2, head1, up[0:ts - 16]], axis=0))

    def neg_conv3(col0, carry_ref, c):
        cols = slice(col0 + c * FFN_CHUNK, col0 + (c + 1) * FFN_CHUNK)
        up = jnp.dot(x1pb, w_up_ref[:, cols], preferred_element_type=F32)
        up1, up2 = causal_taps(up, carry_ref, c)
        w0, w1, w2 = -fw_ref[0:1, cols], -fw_ref[1:2, cols], -fw_ref[2:3, cols]
        return w2 * up + (w1 * up1 + (w0 * up2 - fb_ref[:, cols]))

    for c in range(n_chunks):
        ng = neg_conv3(0, cg_carry_ref, c)
        nu = neg_conv3(d_ff, cu_carry_ref, c)
        act = ng * nu * (1.0 / (1.0 + jnp.exp(ng)))
        act_ref[:, c * FFN_CHUNK:(c + 1) * FFN_CHUNK] = act.astype(BF16)
        emit_mixer(interleavable * (c + 1) // n_chunks)
    ffn = jnp.dot(act_ref[...], wd_ref[...], preferred_element_type=F32)
    for _ in mixer_pieces:
        pass

    x1p = jnp.concatenate([perm_ref[l] for l in range(n_slabs)], axis=1)
    out_p = _layer_norm(alpha * x1p + ffn, l2g_ref[...], l2b_ref[...])

    pitch = nat_ref.shape[1] // 8
    for l in range(n_slabs):
        for v in range(seg):
            nat_ref[l, pl.ds(v, 8, stride=pitch), :] = out_p[v * 8:(v + 1) * 8, l * LANES:(l + 1) * LANES]
    for l in range(n_slabs):
        for s in range(8):
            o_ref[s * seg:(s + 1) * seg, l * LANES:(l + 1) * LANES] = nat_ref[l, s * pitch:s * pitch + seg, :]


def _full_spec(shape):
    return pl.BlockSpec(shape, lambda g: (0,) * len(shape))


def _layer(x, alpha, w_in, b_in, sinks, tbl, cw, cb, clg, clb, ag, cg, w_out, b_out, l1g, l1b,
           w_up, fw, fb, w_down, l2g, l2b, seq_tile):
    bsz, seq, d_model = x.shape
    d_conv = cw.shape[1]
    d_ff = w_down.shape[0]
    d_kv = N_KV_HEADS * HEAD_DIM
    n_chunks = d_ff // FFN_CHUNK
    tiles_per_row = seq // seq_tile
    n_tiles = bsz * tiles_per_row
    seg = seq_tile // 8
    nat_pitch = seg + 8
    assert seq % seq_tile == 0 and seq_tile % WINDOW == 0 and d_ff % FFN_CHUNK == 0
    assert seg % 16 == 0 and d_model % LANES == 0

    row = lambda v: v.reshape(1, -1).astype(F32)
    weights = [w.astype(F32) for w in (w_in, w_out, w_up, w_down)]
    assert all(w.shape[0] % STAGE_ROWS == 0 and w.shape[1] % LANES == 0 for w in weights)
    args = [
        x,
        weights[0], row(b_in), sinks.astype(F32), tbl.reshape(-1).astype(F32),
        cw.astype(F32), row(cb), row(clg), row(clb), row(ag), row(cg),
        weights[1], row(b_out), row(l1g), row(l1b),
        weights[2], fw.astype(F32), row(fb), weights[3], row(l2g), row(l2b),
    ]
    smem = pl.BlockSpec(memory_space=pltpu.SMEM)
    in_hbm = pl.BlockSpec(memory_space=pl.ANY)
    big = {1, 11, 15, 18}

    def mixer_tile(g):
        t = jnp.minimum(g, n_tiles - 1)
        return (t // tiles_per_row, t % tiles_per_row, 0)

    def ffn_tile(g):
        t = jnp.maximum(g - 1, 0)
        return (t // tiles_per_row, t % tiles_per_row, 0)

    in_specs = [pl.BlockSpec((None, seq_tile, d_model), mixer_tile)]
    for idx, a in enumerate(args[1:], start=1):
        in_specs.append(smem if idx in (3, 4) else in_hbm if idx in big else _full_spec(a.shape))

    scratch = [pltpu.VMEM(w.shape, BF16) for w in weights] + [
        pltpu.VMEM((STAGE_SLOTS, STAGE_ROWS, STAGE_COLS), F32),
        pltpu.SemaphoreType.DMA((STAGE_SLOTS,)),
        pltpu.VMEM((N_HEADS, WINDOW, WINDOW), F32),
        pltpu.VMEM((CONV_WIDTH, 8, d_conv), F32),
        pltpu.VMEM((8, seq_tile + WINDOW, d_kv), BF16),
        pltpu.VMEM((seq_tile + CONV_HALO, d_conv), F32),
        pltpu.VMEM((7, seq_tile + CONV_HALO - 8, d_conv), F32),
        pltpu.VMEM((seq_tile, d_model), BF16),
        pltpu.VMEM((seq_tile, d_model), F32),
        pltpu.VMEM((d_model // LANES, seq_tile, LANES), F32),
        pltpu.VMEM((d_model // LANES, 8 * nat_pitch, LANES), F32),
        pltpu.VMEM((seq_tile, d_ff), BF16),
        pltpu.VMEM((n_chunks, 16, FFN_CHUNK), F32),
        pltpu.VMEM((n_chunks, 16, FFN_CHUNK), F32),
    ]
    return pl.pallas_call(
        functools.partial(_layer_kernel, alpha, n_tiles, tiles_per_row),
        grid=(n_tiles + 1,),
        in_specs=in_specs,
        out_specs=pl.BlockSpec((None, seq_tile, d_model), ffn_tile),
        out_shape=jax.ShapeDtypeStruct(x.shape, x.dtype),
        scratch_shapes=scratch,
        compiler_params=pltpu.CompilerParams(
            dimension_semantics=("arbitrary",),
            vmem_limit_bytes=VMEM_LIMIT_BYTES),
        name="hybrid_layer",
    )(*args)


def kernel(x, w_in, b_in, attn_sinks, rel_bias_table, conv_dw_w, conv_dw_b, conv_ln_g, conv_ln_b,
           attn_out_gain, conv_out_gain, w_out, b_out, ln1_g, ln1_b, w_up, ffn_dw_w, ffn_dw_b, w_down,
           ln2_g, ln2_b, seq_tile=SEQ_TILE):
    depth = w_in.shape[0]
    alpha = (2.0 * depth) ** 0.25
    for l in range(depth):
        x = _layer(x, alpha, w_in[l], b_in[l], attn_sinks[l], rel_bias_table, conv_dw_w[l], conv_dw_b[l],
                   conv_ln_g[l], conv_ln_b[l], attn_out_gain[l], conv_out_gain[l], w_out[l], b_out[l],
                   ln1_g[l], ln1_b[l], w_up[l], ffn_dw_w[l], ffn_dw_b[l], w_down[l], ln2_g[l], ln2_b[l],
                   seq_tile)
    return x
```

```python
import functools
import math

import jax
import jax.numpy as jnp
from jax import lax
from jax.experimental import pallas as pl
from jax.experimental.pallas import tpu as pltpu

HEAD_DIM = 64
N_KV_HEADS = 2
GQA_GROUP = 4
N_HEADS = N_KV_HEADS * GQA_GROUP
WINDOW = 128
CONV_WIDTH = 31
N_BUCKETS = 32
MAX_DISTANCE = 128
FFN_CONV_WIDTH = 3
LN_EPS = 1e-5

SEQ_TILE = 512
CONV_HALO = 32
CONV_ROWS = 32
FFN_CHUNK = 256
LANES = 128
W_GLU, W_SHIFT, W_KV, W_Q, W_ATTN, W_CONV = 1300, 900, 500, 400, 560, 1400
STAGE_ROWS, STAGE_COLS = 256, 1792
STAGE_SLOTS = 3
VMEM_LIMIT_BYTES = 61 * 1024 * 1024

BF16 = jnp.bfloat16
F32 = jnp.float32


def _layer_norm(r, g, b):
    mu = jnp.mean(r, axis=-1, keepdims=True)
    d = r - mu
    var = jnp.mean(d * d, axis=-1, keepdims=True)
    return d * lax.rsqrt(var + LN_EPS) * g + b


def _rms_norm(y, g):
    return y * lax.rsqrt(jnp.mean(y * y, axis=-1, keepdims=True) + LN_EPS) * g


def _dot_nt(a, b):
    return lax.dot_general(a, b, (((1,), (1,)), ((), ())), preferred_element_type=F32)


def _layer_kernel(alpha, n_tiles, tiles_per_row,
                  x_ref, w_in_hbm, b_in_ref, sinks_ref, tbl_ref, cw_ref, cb_ref, clg_ref, clb_ref,
                  ag_ref, cg_ref, w_out_hbm, b_out_ref, l1g_ref, l1b_ref,
                  w_up_hbm, fw_ref, fb_ref, wd_hbm, l2g_ref, l2b_ref,
                  o_ref,
                  w_in_ref, w_out_ref, w_up_ref, wd_ref, stage_ref, stage_sem, bias_ref, cwb_ref, kv_ref, hbuf_ref, hshift_ref, y_ref, x1_ref, perm_ref, nat_ref,
                  act_ref, cg_carry_ref, cu_carry_ref):
    ts = x_ref.shape[0]
    d_model = x_ref.shape[1]
    d_attn = N_HEADS * HEAD_DIM
    d_kv = N_KV_HEADS * HEAD_DIM
    d_conv = cw_ref.shape[1]
    d_ff = wd_ref.shape[0]
    n_chunks = d_ff // FFN_CHUNK
    n_slabs = d_model // LANES
    g = pl.program_id(0)
    i = jnp.minimum(g, n_tiles - 1) % tiles_per_row
    fi = jnp.maximum(g - 1, 0) % tiles_per_row

    qi = lax.broadcasted_iota(jnp.int32, (WINDOW, WINDOW), 0)
    kj = lax.broadcasted_iota(jnp.int32, (WINDOW, WINDOW), 1)
    tri = kj <= qi

    def load_weights(pairs):
        n_slots, stage_rows, stage_cols = stage_ref.shape
        blocks = []
        for src_hbm, dst_ref in pairs:
            rows, cols = src_hbm.shape
            width = max(w for w in range(LANES, stage_cols + 1, LANES) if cols % w == 0)
            blocks += [(src_hbm, dst_ref, r0, c0, width)
                       for c0 in range(0, cols, width) for r0 in range(0, rows, stage_rows)]

        def copy(k):
            src_hbm, _, r0, c0, width = blocks[k]
            slot = k % n_slots
            return pltpu.make_async_copy(src_hbm.at[pl.ds(r0, stage_rows), pl.ds(c0, width)],
                                         stage_ref.at[slot, :, pl.ds(0, width)], stage_sem.at[slot])

        for k in range(min(n_slots - 1, len(blocks))):
            copy(k).start()
        for k, (_, dst_ref, r0, c0, width) in enumerate(blocks):
            if k + n_slots - 1 < len(blocks):
                copy(k + n_slots - 1).start()
            copy(k).wait()
            dst_ref[r0:r0 + stage_rows, c0:c0 + width] = stage_ref[k % n_slots, :, 0:width].astype(BF16)

    @pl.when(g == 0)
    def _first_step():
        load_weights(((w_in_hbm, w_in_ref), (w_out_hbm, w_out_ref), (w_up_hbm, w_up_ref), (wd_hbm, wd_ref)))
        x1_ref[...] = jnp.zeros(x1_ref.shape, F32)
        for k in range(CONV_WIDTH):
            cwb_ref[k] = jnp.broadcast_to(cw_ref[k:k + 1, :], (8, d_conv))
        n = (qi - kj) & (WINDOW - 1)
        max_exact = N_BUCKETS // 2
        nf = jnp.maximum(n, max_exact).astype(F32)
        large = max_exact + (jnp.log(nf / max_exact) / math.log(MAX_DISTANCE / max_exact)
                             * (N_BUCKETS - max_exact)).astype(jnp.int32)
        large = jnp.minimum(large, N_BUCKETS - 1)
        bucket = jnp.where(n < max_exact, n, large)
        for hd in range(N_HEADS):
            bias = jnp.zeros((WINDOW, WINDOW), F32)
            for bkt in range(N_BUCKETS):
                bias = jnp.where(bucket == bkt, tbl_ref[bkt * N_HEADS + hd], bias)
            bias_ref[hd] = bias

    @pl.when(i == 0)
    def _reset_mixer_history():
        kv_ref[:, 0:WINDOW, :] = jnp.zeros((8, WINDOW, d_kv), BF16)
        hbuf_ref[0:CONV_HALO, :] = jnp.zeros((CONV_HALO, d_conv), F32)

    @pl.when(fi == 0)
    def _reset_ffn_history():
        cg_carry_ref[...] = jnp.zeros(cg_carry_ref.shape, F32)
        cu_carry_ref[...] = jnp.zeros(cu_carry_ref.shape, F32)

    def mixer():
        x = x_ref[...]
        proj = jnp.dot(x.astype(BF16), w_in_ref[...], preferred_element_type=F32) + b_in_ref[...]
        q_end, k_end, v_end, a_end = d_attn, d_attn + d_kv, d_attn + 2 * d_kv, d_attn + 2 * d_kv + d_conv
        yield 0

        hbuf_ref[CONV_HALO:CONV_HALO + ts, :] = proj[:, v_end:a_end] * jax.nn.sigmoid(proj[:, a_end:])
        yield W_GLU
        for s in range(1, 8):
            hshift_ref[s - 1] = hbuf_ref[s:s + hshift_ref.shape[1], :]
            yield W_SHIFT

        lane = lax.broadcasted_iota(jnp.int32, (ts, d_kv), 1)
        lo = lane < HEAD_DIM
        for base, t in ((0, proj[:, q_end:k_end]), (4, proj[:, k_end:v_end])):
            tr = pltpu.roll(t, HEAD_DIM, axis=1)
            zero = jnp.zeros_like(t)
            kv_ref[base + 0, WINDOW:WINDOW + ts, :] = jnp.where(lo, t, zero).astype(BF16)
            kv_ref[base + 1, WINDOW:WINDOW + ts, :] = jnp.where(lo, zero, tr).astype(BF16)
            kv_ref[base + 2, WINDOW:WINDOW + ts, :] = jnp.where(lo, tr, zero).astype(BF16)
            kv_ref[base + 3, WINDOW:WINDOW + ts, :] = jnp.where(lo, zero, t).astype(BF16)
            yield W_KV
        qs = (proj[:, 0:q_end] * (HEAD_DIM ** -0.5)).astype(BF16)
        yield W_Q

        conv, attn = conv_pieces(), attention_pieces(qs)
        for w in conv:
            yield w
            yield next(conv, 0)
            yield next(attn, 0)
        for w in attn:
            yield w

        mix = jnp.dot(y_ref[...], w_out_ref[...], preferred_element_type=F32) + b_out_ref[...]
        x1_ref[...] = _layer_norm(alpha * x + mix, l1g_ref[...], l1b_ref[...])
        yield 0

    def attention_pieces(qs):
        lo_w = lax.broadcasted_iota(jnp.int32, (WINDOW, 2 * HEAD_DIM), 1) < HEAD_DIM
        units = [(j, h) for j in range(ts // WINDOW) for h in range(N_KV_HEADS)]
        scores, probs, inv, pair_out = {}, {}, {}, {}

        def score_dots(j, h):
            qb = qs[j * WINDOW:(j + 1) * WINDOW, :]
            q2 = jnp.concatenate([qb[:, (2 * h) * LANES:(2 * h + 1) * LANES],
                                  qb[:, (2 * h + 1) * LANES:(2 * h + 2) * LANES]], axis=0)
            scores[j, h] = [_dot_nt(q2, kv_ref[2 * h + ab, j * WINDOW:(j + 2) * WINDOW, :])
                            for ab in range(2)]

        def softmax(j, h):
            has_prev = (i > 0) | (j > 0)
            prev_mask = jnp.where(has_prev, 0.0, -jnp.inf).astype(F32)
            for ab in range(2):
                s = scores[j, h][ab]
                for pp in range(2):
                    hd = 4 * h + 2 * pp + ab
                    s_prev = s[pp * WINDOW:(pp + 1) * WINDOW, 0:WINDOW]
                    s_cur = s[pp * WINDOW:(pp + 1) * WINDOW, WINDOW:2 * WINDOW]
                    sc = jnp.where(tri, s_cur, s_prev + prev_mask) + bias_ref[hd]
                    sink = sinks_ref[hd]
                    m = jnp.maximum(jnp.max(sc, axis=-1, keepdims=True), sink)
                    p = jnp.exp(sc - m)
                    denom = jnp.sum(p, axis=-1, keepdims=True) + jnp.exp(sink - m)
                    zero = jnp.zeros_like(p)
                    probs[j, h, pp, ab] = jnp.concatenate(
                        [jnp.where(tri, zero, p), jnp.where(tri, p, zero)], axis=1).astype(BF16)
                    inv[j, h, pp, ab] = 1.0 / denom

        def weighted_values(j, h):
            rows = slice(j * WINDOW, (j + 2) * WINDOW)
            for pp in range(2):
                o = (jnp.dot(probs[j, h, pp, 0], kv_ref[4 + 2 * h + 0, rows, :], preferred_element_type=F32)
                     + jnp.dot(probs[j, h, pp, 1], kv_ref[4 + 2 * h + 1, rows, :], preferred_element_type=F32))
                pair_out[j, h, pp] = o * jnp.where(lo_w, inv[j, h, pp, 0], inv[j, h, pp, 1])
            if h + 1 == N_KV_HEADS:
                ya = jnp.concatenate([pair_out[j, hh, pp] for hh in range(N_KV_HEADS) for pp in range(2)], axis=1)
                y_ref[j * WINDOW:(j + 1) * WINDOW, 0:d_attn] = _rms_norm(ya, ag_ref[...]).astype(BF16)

        for t in range(len(units) + 2):
            if t >= 2:
                weighted_values(*units[t - 2])
            if 1 <= t <= len(units):
                softmax(*units[t - 1])
            if t < len(units):
                score_dots(*units[t])
            yield W_ATTN
        kv_ref[:, 0:WINDOW, :] = kv_ref[:, ts:ts + WINDOW, :]

    def conv_pieces():
        for c in range(ts // CONV_ROWS):
            r0 = c * CONV_ROWS
            acc = jnp.broadcast_to(cb_ref[...], (CONV_ROWS, d_conv))
            for k in range(CONV_WIDTH):
                off = CONV_HALO - (CONV_WIDTH - 1) + k
                s, aligned = off % 8, off - off % 8
                rows = slice(r0 + aligned, r0 + aligned + CONV_ROWS)
                tap = hbuf_ref[rows, :] if s == 0 else hshift_ref[s - 1, rows, :]
                acc = acc + jnp.concatenate([cwb_ref[k]] * (CONV_ROWS // 8), axis=0) * tap
            hn = _layer_norm(acc, clg_ref[...], clb_ref[...])
            y_ref[r0:r0 + CONV_ROWS, d_attn:d_attn + d_conv] = (
                _rms_norm(hn * jax.nn.sigmoid(hn), cg_ref[...]).astype(BF16))
            yield W_CONV
        hbuf_ref[0:CONV_HALO, :] = hbuf_ref[ts:ts + CONV_HALO, :]

    mixer_pieces = mixer()
    interleavable = (W_GLU + 7 * W_SHIFT + 2 * W_KV + W_Q + ((ts // WINDOW) * N_KV_HEADS + 2) * W_ATTN
                     + (ts // CONV_ROWS) * W_CONV)

    emitted = [0]

    def emit_mixer(target):
        while emitted[0] < target:
            emitted[0] += next(mixer_pieces)

    seg = ts // 8
    x1_prev = x1_ref[...]
    for l in range(n_slabs):
        for s in range(8):
            for u in range(seg // 8):
                n0 = s * seg + 8 * u
                perm_ref[l, pl.ds(seg * u + s, 8, stride=8), :] = x1_prev[n0:n0 + 8, l * LANES:(l + 1) * LANES]
    x1pb = jnp.concatenate([perm_ref[l] for l in range(n_slabs)], axis=1).astype(BF16)
    next(mixer_pieces)

    first_sublane = lax.broadcasted_iota(jnp.int32, (8, FFN_CHUNK), 0) == 0

    def causal_taps(up, carry_ref, c):
        prev = carry_ref[c]
        carry_ref[c] = up[ts - 16:ts]
        roll1 = lambda t: pltpu.roll(t, 1, axis=0)
        head1 = jnp.where(first_sublane, roll1(prev[8:16]), roll1(up[ts - 8:ts]))
        head2 = jnp.where(first_sublane, roll1(prev[0:8]), roll1(up[ts - 16:ts - 8]))
        return (jnp.concatenate([head1, up[0:ts - 8]], axis=0),
                jnp.concatenate([head2, head1, up[0:ts - 16]], axis=0))

    def neg_conv3(col0, carry_ref, c):
        cols = slice(col0 + c * FFN_CHUNK, col0 + (c + 1) * FFN_CHUNK)
        up = jnp.dot(x1pb, w_up_ref[:, cols], preferred_element_type=F32)
        up1, up2 = causal_taps(up, carry_ref, c)
        w0, w1, w2 = -fw_ref[0:1, cols], -fw_ref[1:2, cols], -fw_ref[2:3, cols]
        return w2 * up + (w1 * up1 + (w0 * up2 - fb_ref[:, cols]))

    split = (n_chunks + 1) // 2 * FFN_CHUNK
    ffn = None
    for c in range(n_chunks):
        ng = neg_conv3(0, cg_carry_ref, c)
        nu = neg_conv3(d_ff, cu_carry_ref, c)
        act = ng * nu * (1.0 / (1.0 + jnp.exp(ng)))
        act_ref[:, c * FFN_CHUNK:(c + 1) * FFN_CHUNK] = act.astype(BF16)
        emit_mixer(interleavable * (c + 1) // n_chunks)
        if (c + 1) * FFN_CHUNK == split:
            ffn = jnp.dot(act_ref[:, 0:split], wd_ref[0:split, :], preferred_element_type=F32)
    ffn = ffn + jnp.dot(act_ref[:, split:], wd_ref[split:, :], preferred_element_type=F32)
    for _ in mixer_pieces:
        pass

    x1p = jnp.concatenate([perm_ref[l] for l in range(n_slabs)], axis=1)
    out_p = _layer_norm(alpha * x1p + ffn, l2g_ref[...], l2b_ref[...])

    pitch = nat_ref.shape[1] // 8
    for l in range(n_slabs):
        for v in range(seg):
            nat_ref[l, pl.ds(v, 8, stride=pitch), :] = out_p[v * 8:(v + 1) * 8, l * LANES:(l + 1) * LANES]
    for l in range(n_slabs):
        for s in range(8):
            o_ref[s * seg:(s + 1) * seg, l * LANES:(l + 1) * LANES] = nat_ref[l, s * pitch:s * pitch + seg, :]


def _full_spec(shape):
    return pl.BlockSpec(shape, lambda g: (0,) * len(shape))


def _layer(x, alpha, w_in, b_in, sinks, tbl, cw, cb, clg, clb, ag, cg, w_out, b_out, l1g, l1b,
           w_up, fw, fb, w_down, l2g, l2b, seq_tile):
    bsz, seq, d_model = x.shape
    d_conv = cw.shape[1]
    d_ff = w_down.shape[0]
    d_kv = N_KV_HEADS * HEAD_DIM
    n_chunks = d_ff // FFN_CHUNK
    tiles_per_row = seq // seq_tile
    n_tiles = bsz * tiles_per_row
    seg = seq_tile // 8
    nat_pitch = seg + 8
    assert seq % seq_tile == 0 and seq_tile % WINDOW == 0 and d_ff % FFN_CHUNK == 0
    assert seg % 16 == 0 and d_model % LANES == 0

    row = lambda v: v.reshape(1, -1).astype(F32)
    weights = [w.astype(F32) for w in (w_in, w_out, w_up, w_down)]
    assert all(w.shape[0] % STAGE_ROWS == 0 and w.shape[1] % LANES == 0 for w in weights)
    args = [
        x,
        weights[0], row(b_in), sinks.astype(F32), tbl.reshape(-1).astype(F32),
        cw.astype(F32), row(cb), row(clg), row(clb), row(ag), row(cg),
        weights[1], row(b_out), row(l1g), row(l1b),
        weights[2], fw.astype(F32), row(fb), weights[3], row(l2g), row(l2b),
    ]
    smem = pl.BlockSpec(memory_space=pltpu.SMEM)
    in_hbm = pl.BlockSpec(memory_space=pl.ANY)
    big = {1, 11, 15, 18}

    def mixer_tile(g):
        t = jnp.minimum(g, n_tiles - 1)
        return (t // tiles_per_row, t % tiles_per_row, 0)

    def ffn_tile(g):
        t = jnp.maximum(g - 1, 0)
        return (t // tiles_per_row, t % tiles_per_row, 0)

    in_specs = [pl.BlockSpec((None, seq_tile, d_model), mixer_tile)]
    for idx, a in enumerate(args[1:], start=1):
        in_specs.append(smem if idx in (3, 4) else in_hbm if idx in big else _full_spec(a.shape))

    scratch = [pltpu.VMEM(w.shape, BF16) for w in weights] + [
        pltpu.VMEM((STAGE_SLOTS, STAGE_ROWS, STAGE_COLS), F32),
        pltpu.SemaphoreType.DMA((STAGE_SLOTS,)),
        pltpu.VMEM((N_HEADS, WINDOW, WINDOW), F32),
        pltpu.VMEM((CONV_WIDTH, 8, d_conv), F32),
        pltpu.VMEM((8, seq_tile + WINDOW, d_kv), BF16),
        pltpu.VMEM((seq_tile + CONV_HALO, d_conv), F32),
        pltpu.VMEM((7, seq_tile + CONV_HALO - 8, d_conv), F32),
        pltpu.VMEM((seq_tile, d_model), BF16),
        pltpu.VMEM((seq_tile, d_model), F32),
        pltpu.VMEM((d_model // LANES, seq_tile, LANES), F32),
        pltpu.VMEM((d_model // LANES, 8 * nat_pitch, LANES), F32),
        pltpu.VMEM((seq_tile, d_ff), BF16),
        pltpu.VMEM((n_chunks, 16, FFN_CHUNK), F32),
        pltpu.VMEM((n_chunks, 16, FFN_CHUNK), F32),
    ]
    return pl.pallas_call(
        functools.partial(_layer_kernel, alpha, n_tiles, tiles_per_row),
        grid=(n_tiles + 1,),
        in_specs=in_specs,
        out_specs=pl.BlockSpec((None, seq_tile, d_model), ffn_tile),
        out_shape=jax.ShapeDtypeStruct(x.shape, x.dtype),
        scratch_shapes=scratch,
        compiler_params=pltpu.CompilerParams(
            dimension_semantics=("arbitrary",),
            vmem_limit_bytes=VMEM_LIMIT_BYTES),
        name="hybrid_layer",
    )(*args)


def kernel(x, w_in, b_in, attn_sinks, rel_bias_table, conv_dw_w, conv_dw_b, conv_ln_g, conv_ln_b,
           attn_out_gain, conv_out_gain, w_out, b_out, ln1_g, ln1_b, w_up, ffn_dw_w, ffn_dw_b, w_down,
           ln2_g, ln2_b, seq_tile=SEQ_TILE):
    depth = w_in.shape[0]
    alpha = (2.0 * depth) ** 0.25
    for l in range(depth):
        x = _layer(x, alpha, w_in[l], b_in[l], attn_sinks[l], rel_bias_table, conv_dw_w[l], conv_dw_b[l],
                   conv_ln_g[l], conv_ln_b[l], attn_out_gain[l], conv_out_gain[l], w_out[l], b_out[l],
                   ln1_g[l], ln1_b[l], w_up[l], ffn_dw_w[l], ffn_dw_b[l], w_down[l], ln2_g[l], ln2_b[l],
                   seq_tile)
    return x
```

```python
import functools
import math

import jax
import jax.numpy as jnp
from jax import lax
from jax.experimental import pallas as pl
from jax.experimental.pallas import tpu as pltpu

HEAD_DIM = 64
N_KV_HEADS = 2
GQA_GROUP = 4
N_HEADS = N_KV_HEADS * GQA_GROUP
WINDOW = 128
CONV_WIDTH = 31
N_BUCKETS = 32
MAX_DISTANCE = 128
FFN_CONV_WIDTH = 3
LN_EPS = 1e-5

SEQ_TILE = 512
CONV_HALO = 32
CONV_ROWS = 32
FFN_CHUNK = 256
DOWN_SPLIT_CHUNKS = 8
LANES = 128
W_GLU, W_SHIFT, W_KV, W_Q, W_ATTN, W_CONV = 1300, 900, 500, 400, 560, 1400
STAGE_ROWS, STAGE_COLS = 256, 1792
STAGE_SLOTS = 4
VMEM_LIMIT_BYTES = 61 * 1024 * 1024

BF16 = jnp.bfloat16
F32 = jnp.float32


def _layer_norm(r, g, b):
    mu = jnp.mean(r, axis=-1, keepdims=True)
    d = r - mu
    var = jnp.mean(d * d, axis=-1, keepdims=True)
    return d * lax.rsqrt(var + LN_EPS) * g + b


def _rms_norm(y, g):
    return y * lax.rsqrt(jnp.mean(y * y, axis=-1, keepdims=True) + LN_EPS) * g


def _dot_nt(a, b):
    return lax.dot_general(a, b, (((1,), (1,)), ((), ())), preferred_element_type=F32)


def _layer_kernel(alpha, n_tiles, tiles_per_row,
                  x_ref, w_in_hbm, b_in_ref, sinks_ref, tbl_ref, cw_ref, cb_ref, clg_ref, clb_ref,
                  ag_ref, cg_ref, w_out_hbm, b_out_ref, l1g_ref, l1b_ref,
                  w_up_hbm, fw_ref, fb_ref, wd_hbm, l2g_ref, l2b_ref,
                  o_ref,
                  w_in_ref, w_out_ref, w_up_ref, wd_ref, stage_ref, stage_sem, bias_ref, cwb_ref, kv_ref, hbuf_ref, hshift_ref, y_ref, x1_ref, perm_ref, nat_ref,
                  act_ref, cg_carry_ref, cu_carry_ref):
    ts = x_ref.shape[0]
    d_model = x_ref.shape[1]
    d_attn = N_HEADS * HEAD_DIM
    d_kv = N_KV_HEADS * HEAD_DIM
    d_conv = cw_ref.shape[1]
    d_ff = wd_ref.shape[0]
    n_chunks = d_ff // FFN_CHUNK
    n_slabs = d_model // LANES
    g = pl.program_id(0)
    i = jnp.minimum(g, n_tiles - 1) % tiles_per_row
    fi = jnp.maximum(g - 1, 0) % tiles_per_row

    qi = lax.broadcasted_iota(jnp.int32, (WINDOW, WINDOW), 0)
    kj = lax.broadcasted_iota(jnp.int32, (WINDOW, WINDOW), 1)
    tri = kj <= qi

    def load_weights(pairs):
        n_slots, stage_rows, stage_cols = stage_ref.shape
        blocks = []
        for src_hbm, dst_ref in pairs:
            rows, cols = src_hbm.shape
            width = max(w for w in range(LANES, stage_cols + 1, LANES) if cols % w == 0)
            blocks += [(src_hbm, dst_ref, r0, c0, width)
                       for c0 in range(0, cols, width) for r0 in range(0, rows, stage_rows)]

        def copy(k):
            src_hbm, _, r0, c0, width = blocks[k]
            slot = k % n_slots
            return pltpu.make_async_copy(src_hbm.at[pl.ds(r0, stage_rows), pl.ds(c0, width)],
                                         stage_ref.at[slot, :, pl.ds(0, width)], stage_sem.at[slot])

        for k in range(min(n_slots - 1, len(blocks))):
            copy(k).start()
        for k, (_, dst_ref, r0, c0, width) in enumerate(blocks):
            if k + n_slots - 1 < len(blocks):
                copy(k + n_slots - 1).start()
            copy(k).wait()
            dst_ref[r0:r0 + stage_rows, c0:c0 + width] = stage_ref[k % n_slots, :, 0:width].astype(BF16)

    @pl.when(g == 0)
    def _first_step():
        load_weights(((w_in_hbm, w_in_ref), (w_out_hbm, w_out_ref), (w_up_hbm, w_up_ref), (wd_hbm, wd_ref)))
        x1_ref[...] = jnp.zeros(x1_ref.shape, F32)
        for k in range(CONV_WIDTH):
            cwb_ref[k] = jnp.broadcast_to(cw_ref[k:k + 1, :], (8, d_conv))
        n = (qi - kj) & (WINDOW - 1)
        max_exact = N_BUCKETS // 2
        nf = jnp.maximum(n, max_exact).astype(F32)
        large = max_exact + (jnp.log(nf / max_exact) / math.log(MAX_DISTANCE / max_exact)
                             * (N_BUCKETS - max_exact)).astype(jnp.int32)
        large = jnp.minimum(large, N_BUCKETS - 1)
        bucket = jnp.where(n < max_exact, n, large)
        for hd in range(N_HEADS):
            bias = jnp.zeros((WINDOW, WINDOW), F32)
            for bkt in range(N_BUCKETS):
                bias = jnp.where(bucket == bkt, tbl_ref[bkt * N_HEADS + hd], bias)
            bias_ref[hd] = bias

    @pl.when(i == 0)
    def _reset_mixer_history():
        kv_ref[:, 0:WINDOW, :] = jnp.zeros((8, WINDOW, d_kv), BF16)
        hbuf_ref[0:CONV_HALO, :] = jnp.zeros((CONV_HALO, d_conv), F32)

    @pl.when(fi == 0)
    def _reset_ffn_history():
        cg_carry_ref[...] = jnp.zeros(cg_carry_ref.shape, F32)
        cu_carry_ref[...] = jnp.zeros(cu_carry_ref.shape, F32)

    def mixer():
        x = x_ref[...]
        proj = jnp.dot(x.astype(BF16), w_in_ref[...], preferred_element_type=F32) + b_in_ref[...]
        q_end, k_end, v_end, a_end = d_attn, d_attn + d_kv, d_attn + 2 * d_kv, d_attn + 2 * d_kv + d_conv
        yield 0

        hbuf_ref[CONV_HALO:CONV_HALO + ts, :] = proj[:, v_end:a_end] * jax.nn.sigmoid(proj[:, a_end:])
        yield W_GLU
        for s in range(1, 8):
            hshift_ref[s - 1] = hbuf_ref[s:s + hshift_ref.shape[1], :]
            yield W_SHIFT

        lane = lax.broadcasted_iota(jnp.int32, (ts, d_kv), 1)
        lo = lane < HEAD_DIM
        for base, t in ((0, proj[:, q_end:k_end]), (4, proj[:, k_end:v_end])):
            tr = pltpu.roll(t, HEAD_DIM, axis=1)
            zero = jnp.zeros_like(t)
            kv_ref[base + 0, WINDOW:WINDOW + ts, :] = jnp.where(lo, t, zero).astype(BF16)
            kv_ref[base + 1, WINDOW:WINDOW + ts, :] = jnp.where(lo, zero, tr).astype(BF16)
            kv_ref[base + 2, WINDOW:WINDOW + ts, :] = jnp.where(lo, tr, zero).astype(BF16)
            kv_ref[base + 3, WINDOW:WINDOW + ts, :] = jnp.where(lo, zero, t).astype(BF16)
            yield W_KV
        qs = (proj[:, 0:q_end] * (HEAD_DIM ** -0.5)).astype(BF16)
        yield W_Q

        conv, attn = conv_pieces(), attention_pieces(qs)
        for w in conv:
            yield w
            yield next(conv, 0)
            yield next(attn, 0)
        for w in attn:
            yield w

        mix = jnp.dot(y_ref[...], w_out_ref[...], preferred_element_type=F32) + b_out_ref[...]
        x1_ref[...] = _layer_norm(alpha * x + mix, l1g_ref[...], l1b_ref[...])
        yield 0

    def attention_pieces(qs):
        lo_w = lax.broadcasted_iota(jnp.int32, (WINDOW, 2 * HEAD_DIM), 1) < HEAD_DIM
        units = [(j, h) for j in range(ts // WINDOW) for h in range(N_KV_HEADS)]
        scores, probs, inv, pair_out = {}, {}, {}, {}

        def score_dots(j, h):
            qb = qs[j * WINDOW:(j + 1) * WINDOW, :]
            q2 = jnp.concatenate([qb[:, (2 * h) * LANES:(2 * h + 1) * LANES],
                                  qb[:, (2 * h + 1) * LANES:(2 * h + 2) * LANES]], axis=0)
            scores[j, h] = [_dot_nt(q2, kv_ref[2 * h + ab, j * WINDOW:(j + 2) * WINDOW, :])
                            for ab in range(2)]

        def softmax(j, h):
            has_prev = (i > 0) | (j > 0)
            prev_mask = jnp.where(has_prev, 0.0, -jnp.inf).astype(F32)
            for ab in range(2):
                s = scores[j, h][ab]
                for pp in range(2):
                    hd = 4 * h + 2 * pp + ab
                    s_prev = s[pp * WINDOW:(pp + 1) * WINDOW, 0:WINDOW]
                    s_cur = s[pp * WINDOW:(pp + 1) * WINDOW, WINDOW:2 * WINDOW]
                    sc = jnp.where(tri, s_cur, s_prev + prev_mask) + bias_ref[hd]
                    sink = sinks_ref[hd]
                    m = jnp.maximum(jnp.max(sc, axis=-1, keepdims=True), sink)
                    p = jnp.exp(sc - m)
                    denom = jnp.sum(p, axis=-1, keepdims=True) + jnp.exp(sink - m)
                    zero = jnp.zeros_like(p)
                    probs[j, h, pp, ab] = jnp.concatenate(
                        [jnp.where(tri, zero, p), jnp.where(tri, p, zero)], axis=1).astype(BF16)
                    inv[j, h, pp, ab] = 1.0 / denom

        def weighted_values(j, h):
            rows = slice(j * WINDOW, (j + 2) * WINDOW)
            for pp in range(2):
                o = (jnp.dot(probs[j, h, pp, 0], kv_ref[4 + 2 * h + 0, rows, :], preferred_element_type=F32)
                     + jnp.dot(probs[j, h, pp, 1], kv_ref[4 + 2 * h + 1, rows, :], preferred_element_type=F32))
                pair_out[j, h, pp] = o * jnp.where(lo_w, inv[j, h, pp, 0], inv[j, h, pp, 1])
            if h + 1 == N_KV_HEADS:
                ya = jnp.concatenate([pair_out[j, hh, pp] for hh in range(N_KV_HEADS) for pp in range(2)], axis=1)
                y_ref[j * WINDOW:(j + 1) * WINDOW, 0:d_attn] = _rms_norm(ya, ag_ref[...]).astype(BF16)

        for t in range(len(units) + 2):
            if t >= 2:
                weighted_values(*units[t - 2])
            if 1 <= t <= len(units):
                softmax(*units[t - 1])
            if t < len(units):
                score_dots(*units[t])
            yield W_ATTN
        kv_ref[:, 0:WINDOW, :] = kv_ref[:, ts:ts + WINDOW, :]

    def conv_pieces():
        for c in range(ts // CONV_ROWS):
            r0 = c * CONV_ROWS
            acc = jnp.broadcast_to(cb_ref[...], (CONV_ROWS, d_conv))
            for k in range(CONV_WIDTH):
                off = CONV_HALO - (CONV_WIDTH - 1) + k
                s, aligned = off % 8, off - off % 8
                rows = slice(r0 + aligned, r0 + aligned + CONV_ROWS)
                tap = hbuf_ref[rows, :] if s == 0 else hshift_ref[s - 1, rows, :]
                acc = acc + jnp.concatenate([cwb_ref[k]] * (CONV_ROWS // 8), axis=0) * tap
            hn = _layer_norm(acc, clg_ref[...], clb_ref[...])
            y_ref[r0:r0 + CONV_ROWS, d_attn:d_attn + d_conv] = (
                _rms_norm(hn * jax.nn.sigmoid(hn), cg_ref[...]).astype(BF16))
            yield W_CONV
        hbuf_ref[0:CONV_HALO, :] = hbuf_ref[ts:ts + CONV_HALO, :]

    mixer_pieces = mixer()
    interleavable = (W_GLU + 7 * W_SHIFT + 2 * W_KV + W_Q + ((ts // WINDOW) * N_KV_HEADS + 2) * W_ATTN
                     + (ts // CONV_ROWS) * W_CONV)

    emitted = [0]

    def emit_mixer(target):
        while emitted[0] < target:
            emitted[0] += next(mixer_pieces)

    seg = ts // 8
    x1_prev = x1_ref[...]
    for l in range(n_slabs):
        for s in range(8):
            for u in range(seg // 8):
                n0 = s * seg + 8 * u
                perm_ref[l, pl.ds(seg * u + s, 8, stride=8), :] = x1_prev[n0:n0 + 8, l * LANES:(l + 1) * LANES]
    x1pb = jnp.concatenate([perm_ref[l] for l in range(n_slabs)], axis=1).astype(BF16)
    next(mixer_pieces)

    first_sublane = lax.broadcasted_iota(jnp.int32, (8, FFN_CHUNK), 0) == 0

    def causal_taps(up, carry_ref, c):
        prev = carry_ref[c]
        carry_ref[c] = up[ts - 16:ts]
        roll1 = lambda t: pltpu.roll(t, 1, axis=0)
        head1 = jnp.where(first_sublane, roll1(prev[8:16]), roll1(up[ts - 8:ts]))
        head2 = jnp.where(first_sublane, roll1(prev[0:8]), roll1(up[ts - 16:ts - 8]))
        return (jnp.concatenate([head1, up[0:ts - 8]], axis=0),
                jnp.concatenate([head2, head1, up[0:ts - 16]], axis=0))

    def neg_conv3(col0, carry_ref, c):
        cols = slice(col0 + c * FFN_CHUNK, col0 + (c + 1) * FFN_CHUNK)
        up = jnp.dot(x1pb, w_up_ref[:, cols], preferred_element_type=F32)
        up1, up2 = causal_taps(up, carry_ref, c)
        w0, w1, w2 = -fw_ref[0:1, cols], -fw_ref[1:2, cols], -fw_ref[2:3, cols]
        return w2 * up + (w1 * up1 + (w0 * up2 - fb_ref[:, cols]))

    split = DOWN_SPLIT_CHUNKS * FFN_CHUNK
    ffn = None
    for c in range(n_chunks):
        ng = neg_conv3(0, cg_carry_ref, c)
        nu = neg_conv3(d_ff, cu_carry_ref, c)
        act = ng * nu * (1.0 / (1.0 + jnp.exp(ng)))
        act_ref[:, c * FFN_CHUNK:(c + 1) * FFN_CHUNK] = act.astype(BF16)
        emit_mixer(interleavable * (c + 1) // n_chunks)
        if (c + 1) * FFN_CHUNK == split:
            ffn = jnp.dot(act_ref[:, 0:split], wd_ref[0:split, :], preferred_element_type=F32)
    ffn = ffn + jnp.dot(act_ref[:, split:], wd_ref[split:, :], preferred_element_type=F32)
    for _ in mixer_pieces:
        pass

    x1p = jnp.concatenate([perm_ref[l] for l in range(n_slabs)], axis=1)
    out_p = _layer_norm(alpha * x1p + ffn, l2g_ref[...], l2b_ref[...])

    pitch = nat_ref.shape[1] // 8
    for l in range(n_slabs):
        for v in range(seg):
            nat_ref[l, pl.ds(v, 8, stride=pitch), :] = out_p[v * 8:(v + 1) * 8, l * LANES:(l + 1) * LANES]
    for l in range(n_slabs):
        for s in range(8):
            o_ref[s * seg:(s + 1) * seg, l * LANES:(l + 1) * LANES] = nat_ref[l, s * pitch:s * pitch + seg, :]


def _full_spec(shape):
    return pl.BlockSpec(shape, lambda g: (0,) * len(shape))


def _layer(x, alpha, w_in, b_in, sinks, tbl, cw, cb, clg, clb, ag, cg, w_out, b_out, l1g, l1b,
           w_up, fw, fb, w_down, l2g, l2b, seq_tile):
    bsz, seq, d_model = x.shape
    d_conv = cw.shape[1]
    d_ff = w_down.shape[0]
    d_kv = N_KV_HEADS * HEAD_DIM
    n_chunks = d_ff // FFN_CHUNK
    tiles_per_row = seq // seq_tile
    n_tiles = bsz * tiles_per_row
    seg = seq_tile // 8
    nat_pitch = seg + 8
    assert seq % seq_tile == 0 and seq_tile % WINDOW == 0 and d_ff % FFN_CHUNK == 0
    assert seg % 16 == 0 and d_model % LANES == 0

    row = lambda v: v.reshape(1, -1).astype(F32)
    weights = [w.astype(F32) for w in (w_in, w_out, w_up, w_down)]
    assert all(w.shape[0] % STAGE_ROWS == 0 and w.shape[1] % LANES == 0 for w in weights)
    args = [
        x,
        weights[0], row(b_in), sinks.astype(F32), tbl.reshape(-1).astype(F32),
        cw.astype(F32), row(cb), row(clg), row(clb), row(ag), row(cg),
        weights[1], row(b_out), row(l1g), row(l1b),
        weights[2], fw.astype(F32), row(fb), weights[3], row(l2g), row(l2b),
    ]
    smem = pl.BlockSpec(memory_space=pltpu.SMEM)
    in_hbm = pl.BlockSpec(memory_space=pl.ANY)
    big = {1, 11, 15, 18}

    def mixer_tile(g):
        t = jnp.minimum(g, n_tiles - 1)
        return (t // tiles_per_row, t % tiles_per_row, 0)

    def ffn_tile(g):
        t = jnp.maximum(g - 1, 0)
        return (t // tiles_per_row, t % tiles_per_row, 0)

    in_specs = [pl.BlockSpec((None, seq_tile, d_model), mixer_tile)]
    for idx, a in enumerate(args[1:], start=1):
        in_specs.append(smem if idx in (3, 4) else in_hbm if idx in big else _full_spec(a.shape))

    scratch = [pltpu.VMEM(w.shape, BF16) for w in weights] + [
        pltpu.VMEM((STAGE_SLOTS, STAGE_ROWS, STAGE_COLS), F32),
        pltpu.SemaphoreType.DMA((STAGE_SLOTS,)),
        pltpu.VMEM((N_HEADS, WINDOW, WINDOW), F32),
        pltpu.VMEM((CONV_WIDTH, 8, d_conv), F32),
        pltpu.VMEM((8, seq_tile + WINDOW, d_kv), BF16),
        pltpu.VMEM((seq_tile + CONV_HALO, d_conv), F32),
        pltpu.VMEM((7, seq_tile + CONV_HALO - 8, d_conv), F32),
        pltpu.VMEM((seq_tile, d_model), BF16),
        pltpu.VMEM((seq_tile, d_model), F32),
        pltpu.VMEM((d_model // LANES, seq_tile, LANES), F32),
        pltpu.VMEM((d_model // LANES, 8 * nat_pitch, LANES), F32),
        pltpu.VMEM((seq_tile, d_ff), BF16),
        pltpu.VMEM((n_chunks, 16, FFN_CHUNK), F32),
        pltpu.VMEM((n_chunks, 16, FFN_CHUNK), F32),
    ]
    return pl.pallas_call(
        functools.partial(_layer_kernel, alpha, n_tiles, tiles_per_row),
        grid=(n_tiles + 1,),
        in_specs=in_specs,
        out_specs=pl.BlockSpec((None, seq_tile, d_model), ffn_tile),
        out_shape=jax.ShapeDtypeStruct(x.shape, x.dtype),
        scratch_shapes=scratch,
        compiler_params=pltpu.CompilerParams(
            dimension_semantics=("arbitrary",),
            vmem_limit_bytes=VMEM_LIMIT_BYTES),
        name="hybrid_layer",
    )(*args)


def kernel(x, w_in, b_in, attn_sinks, rel_bias_table, conv_dw_w, conv_dw_b, conv_ln_g, conv_ln_b,
           attn_out_gain, conv_out_gain, w_out, b_out, ln1_g, ln1_b, w_up, ffn_dw_w, ffn_dw_b, w_down,
           ln2_g, ln2_b, seq_tile=SEQ_TILE):
    depth = w_in.shape[0]
    alpha = (2.0 * depth) ** 0.25
    for l in range(depth):
        x = _layer(x, alpha, w_in[l], b_in[l], attn_sinks[l], rel_bias_table, conv_dw_w[l], conv_dw_b[l],
                   conv_ln_g[l], conv_ln_b[l], attn_out_gain[l], conv_out_gain[l], w_out[l], b_out[l],
                   ln1_g[l], ln1_b[l], w_up[l], ffn_dw_w[l], ffn_dw_b[l], w_down[l], ln2_g[l], ln2_b[l],
                   seq_tile)
    return x
```

```python
import functools
import math

import jax
import jax.numpy as jnp
from jax import lax
from jax.experimental import pallas as pl
from jax.experimental.pallas import tpu as pltpu

HEAD_DIM = 64
N_KV_HEADS = 2
GQA_GROUP = 4
N_HEADS = N_KV_HEADS * GQA_GROUP
WINDOW = 128
CONV_WIDTH = 31
N_BUCKETS = 32
MAX_DISTANCE = 128
FFN_CONV_WIDTH = 3
LN_EPS = 1e-5

SEQ_TILE = 256
CONV_HALO = 32
CONV_ROWS = 32
FFN_CHUNK = 256
DOWN_SPLIT_CHUNKS = 8
LANES = 128
W_GLU, W_SHIFT, W_KV, W_Q, W_ATTN, W_CONV = 1300, 900, 500, 400, 560, 1400
STAGE_ROWS, STAGE_COLS = 256, 1792
STAGE_SLOTS = 4
VMEM_LIMIT_BYTES = 61 * 1024 * 1024

BF16 = jnp.bfloat16
F32 = jnp.float32


def _layer_norm(r, g, b):
    mu = jnp.mean(r, axis=-1, keepdims=True)
    d = r - mu
    var = jnp.mean(d * d, axis=-1, keepdims=True)
    return d * lax.rsqrt(var + LN_EPS) * g + b


def _rms_norm(y, g):
    return y * lax.rsqrt(jnp.mean(y * y, axis=-1, keepdims=True) + LN_EPS) * g


def _dot_nt(a, b):
    return lax.dot_general(a, b, (((1,), (1,)), ((), ())), preferred_element_type=F32)


def _layer_kernel(alpha, n_tiles, tiles_per_row,
                  x_ref, w_in_hbm, b_in_ref, sinks_ref, tbl_ref, cw_ref, cb_ref, clg_ref, clb_ref,
                  ag_ref, cg_ref, w_out_hbm, b_out_ref, l1g_ref, l1b_ref,
                  w_up_hbm, fw_ref, fb_ref, wd_hbm, l2g_ref, l2b_ref,
                  o_ref,
                  w_in_ref, w_out_ref, w_up_ref, wd_ref, stage_ref, stage_sem, bias_ref, cwb_ref, kv_ref, hbuf_ref, hshift_ref, y_ref, x1_ref, perm_ref, nat_ref,
                  act_ref, cg_carry_ref, cu_carry_ref):
    ts = x_ref.shape[0]
    d_model = x_ref.shape[1]
    d_attn = N_HEADS * HEAD_DIM
    d_kv = N_KV_HEADS * HEAD_DIM
    d_conv = cw_ref.shape[1]
    d_ff = wd_ref.shape[0]
    n_chunks = d_ff // FFN_CHUNK
    n_slabs = d_model // LANES
    g = pl.program_id(0)
    i = jnp.minimum(g, n_tiles - 1) % tiles_per_row
    fi = jnp.maximum(g - 1, 0) % tiles_per_row

    qi = lax.broadcasted_iota(jnp.int32, (WINDOW, WINDOW), 0)
    kj = lax.broadcasted_iota(jnp.int32, (WINDOW, WINDOW), 1)
    tri = kj <= qi

    def load_weights(pairs):
        n_slots, stage_rows, stage_cols = stage_ref.shape
        blocks = []
        for src_hbm, dst_ref in pairs:
            rows, cols = src_hbm.shape
            width = max(w for w in range(LANES, stage_cols + 1, LANES) if cols % w == 0)
            blocks += [(src_hbm, dst_ref, r0, c0, width)
                       for c0 in range(0, cols, width) for r0 in range(0, rows, stage_rows)]

        def copy(k):
            src_hbm, _, r0, c0, width = blocks[k]
            slot = k % n_slots
            return pltpu.make_async_copy(src_hbm.at[pl.ds(r0, stage_rows), pl.ds(c0, width)],
                                         stage_ref.at[slot, :, pl.ds(0, width)], stage_sem.at[slot])

        for k in range(min(n_slots - 1, len(blocks))):
            copy(k).start()
        for k, (_, dst_ref, r0, c0, width) in enumerate(blocks):
            if k + n_slots - 1 < len(blocks):
                copy(k + n_slots - 1).start()
            copy(k).wait()
            dst_ref[r0:r0 + stage_rows, c0:c0 + width] = stage_ref[k % n_slots, :, 0:width].astype(BF16)

    @pl.when(g == 0)
    def _first_step():
        load_weights(((w_in_hbm, w_in_ref), (w_out_hbm, w_out_ref), (w_up_hbm, w_up_ref), (wd_hbm, wd_ref)))
        x1_ref[...] = jnp.zeros(x1_ref.shape, F32)
        for k in range(CONV_WIDTH):
            cwb_ref[k] = jnp.broadcast_to(cw_ref[k:k + 1, :], (8, d_conv))
        n = (qi - kj) & (WINDOW - 1)
        max_exact = N_BUCKETS // 2
        nf = jnp.maximum(n, max_exact).astype(F32)
        large = max_exact + (jnp.log(nf / max_exact) / math.log(MAX_DISTANCE / max_exact)
                             * (N_BUCKETS - max_exact)).astype(jnp.int32)
        large = jnp.minimum(large, N_BUCKETS - 1)
        bucket = jnp.where(n < max_exact, n, large)
        for hd in range(N_HEADS):
            bias = jnp.zeros((WINDOW, WINDOW), F32)
            for bkt in range(N_BUCKETS):
                bias = jnp.where(bucket == bkt, tbl_ref[bkt * N_HEADS + hd], bias)
            bias_ref[hd] = bias

    @pl.when(i == 0)
    def _reset_mixer_history():
        kv_ref[:, 0:WINDOW, :] = jnp.zeros((8, WINDOW, d_kv), BF16)
        hbuf_ref[0:CONV_HALO, :] = jnp.zeros((CONV_HALO, d_conv), F32)

    @pl.when(fi == 0)
    def _reset_ffn_history():
        cg_carry_ref[...] = jnp.zeros(cg_carry_ref.shape, F32)
        cu_carry_ref[...] = jnp.zeros(cu_carry_ref.shape, F32)

    def mixer():
        x = x_ref[...]
        proj = jnp.dot(x.astype(BF16), w_in_ref[...], preferred_element_type=F32) + b_in_ref[...]
        q_end, k_end, v_end, a_end = d_attn, d_attn + d_kv, d_attn + 2 * d_kv, d_attn + 2 * d_kv + d_conv
        yield 0

        hbuf_ref[CONV_HALO:CONV_HALO + ts, :] = proj[:, v_end:a_end] * jax.nn.sigmoid(proj[:, a_end:])
        yield W_GLU
        for s in range(1, 8):
            hshift_ref[s - 1] = hbuf_ref[s:s + hshift_ref.shape[1], :]
            yield W_SHIFT

        lane = lax.broadcasted_iota(jnp.int32, (ts, d_kv), 1)
        lo = lane < HEAD_DIM
        for base, t in ((0, proj[:, q_end:k_end]), (4, proj[:, k_end:v_end])):
            tr = pltpu.roll(t, HEAD_DIM, axis=1)
            zero = jnp.zeros_like(t)
            kv_ref[base + 0, WINDOW:WINDOW + ts, :] = jnp.where(lo, t, zero).astype(BF16)
            kv_ref[base + 1, WINDOW:WINDOW + ts, :] = jnp.where(lo, zero, tr).astype(BF16)
            kv_ref[base + 2, WINDOW:WINDOW + ts, :] = jnp.where(lo, tr, zero).astype(BF16)
            kv_ref[base + 3, WINDOW:WINDOW + ts, :] = jnp.where(lo, zero, t).astype(BF16)
            yield W_KV
        qs = (proj[:, 0:q_end] * (HEAD_DIM ** -0.5)).astype(BF16)
        yield W_Q

        conv, attn = conv_pieces(), attention_pieces(qs)
        for w in conv:
            yield w
            yield next(conv, 0)
            yield next(attn, 0)
        for w in attn:
            yield w

        mix = jnp.dot(y_ref[...], w_out_ref[...], preferred_element_type=F32) + b_out_ref[...]
        x1_ref[...] = _layer_norm(alpha * x + mix, l1g_ref[...], l1b_ref[...])
        yield 0

    def attention_pieces(qs):
        lo_w = lax.broadcasted_iota(jnp.int32, (WINDOW, 2 * HEAD_DIM), 1) < HEAD_DIM
        units = [(j, h) for j in range(ts // WINDOW) for h in range(N_KV_HEADS)]
        scores, probs, inv, pair_out = {}, {}, {}, {}

        def score_dots(j, h):
            qb = qs[j * WINDOW:(j + 1) * WINDOW, :]
            q2 = jnp.concatenate([qb[:, (2 * h) * LANES:(2 * h + 1) * LANES],
                                  qb[:, (2 * h + 1) * LANES:(2 * h + 2) * LANES]], axis=0)
            scores[j, h] = [_dot_nt(q2, kv_ref[2 * h + ab, j * WINDOW:(j + 2) * WINDOW, :])
                            for ab in range(2)]

        def softmax(j, h):
            has_prev = (i > 0) | (j > 0)
            prev_mask = jnp.where(has_prev, 0.0, -jnp.inf).astype(F32)
            for ab in range(2):
                s = scores[j, h][ab]
                for pp in range(2):
                    hd = 4 * h + 2 * pp + ab
                    s_prev = s[pp * WINDOW:(pp + 1) * WINDOW, 0:WINDOW]
                    s_cur = s[pp * WINDOW:(pp + 1) * WINDOW, WINDOW:2 * WINDOW]
                    sc = jnp.where(tri, s_cur, s_prev + prev_mask) + bias_ref[hd]
                    sink = sinks_ref[hd]
                    m = jnp.maximum(jnp.max(sc, axis=-1, keepdims=True), sink)
                    p = jnp.exp(sc - m)
                    denom = jnp.sum(p, axis=-1, keepdims=True) + jnp.exp(sink - m)
                    zero = jnp.zeros_like(p)
                    probs[j, h, pp, ab] = jnp.concatenate(
                        [jnp.where(tri, zero, p), jnp.where(tri, p, zero)], axis=1).astype(BF16)
                    inv[j, h, pp, ab] = 1.0 / denom

        def weighted_values(j, h):
            rows = slice(j * WINDOW, (j + 2) * WINDOW)
            for pp in range(2):
                o = (jnp.dot(probs[j, h, pp, 0], kv_ref[4 + 2 * h + 0, rows, :], preferred_element_type=F32)
                     + jnp.dot(probs[j, h, pp, 1], kv_ref[4 + 2 * h + 1, rows, :], preferred_element_type=F32))
                pair_out[j, h, pp] = o * jnp.where(lo_w, inv[j, h, pp, 0], inv[j, h, pp, 1])
            if h + 1 == N_KV_HEADS:
                ya = jnp.concatenate([pair_out[j, hh, pp] for hh in range(N_KV_HEADS) for pp in range(2)], axis=1)
                y_ref[j * WINDOW:(j + 1) * WINDOW, 0:d_attn] = _rms_norm(ya, ag_ref[...]).astype(BF16)

        for t in range(len(units) + 2):
            if t >= 2:
                weighted_values(*units[t - 2])
            if 1 <= t <= len(units):
                softmax(*units[t - 1])
            if t < len(units):
                score_dots(*units[t])
            yield W_ATTN
        kv_ref[:, 0:WINDOW, :] = kv_ref[:, ts:ts + WINDOW, :]

    def conv_pieces():
        for c in range(ts // CONV_ROWS):
            r0 = c * CONV_ROWS
            acc = jnp.broadcast_to(cb_ref[...], (CONV_ROWS, d_conv))
            for k in range(CONV_WIDTH):
                off = CONV_HALO - (CONV_WIDTH - 1) + k
                s, aligned = off % 8, off - off % 8
                rows = slice(r0 + aligned, r0 + aligned + CONV_ROWS)
                tap = hbuf_ref[rows, :] if s == 0 else hshift_ref[s - 1, rows, :]
                acc = acc + jnp.concatenate([cwb_ref[k]] * (CONV_ROWS // 8), axis=0) * tap
            hn = _layer_norm(acc, clg_ref[...], clb_ref[...])
            y_ref[r0:r0 + CONV_ROWS, d_attn:d_attn + d_conv] = (
                _rms_norm(hn * jax.nn.sigmoid(hn), cg_ref[...]).astype(BF16))
            yield W_CONV
        hbuf_ref[0:CONV_HALO, :] = hbuf_ref[ts:ts + CONV_HALO, :]

    mixer_pieces = mixer()
    interleavable = (W_GLU + 7 * W_SHIFT + 2 * W_KV + W_Q + ((ts // WINDOW) * N_KV_HEADS + 2) * W_ATTN
                     + (ts // CONV_ROWS) * W_CONV)

    emitted = [0]

    def emit_mixer(target):
        while emitted[0] < target:
            emitted[0] += next(mixer_pieces)

    seg = ts // 8
    x1_prev = x1_ref[...]
    for l in range(n_slabs):
        for s in range(8):
            for u in range(seg // 8):
                n0 = s * seg + 8 * u
                perm_ref[l, pl.ds(64 * u + s, 8, stride=8), :] = x1_prev[n0:n0 + 8, l * LANES:(l + 1) * LANES]
    x1pb = jnp.concatenate([perm_ref[l] for l in range(n_slabs)], axis=1).astype(BF16)
    next(mixer_pieces)

    first_sublane = lax.broadcasted_iota(jnp.int32, (8, FFN_CHUNK), 0) == 0

    def causal_taps(up, carry_ref, c):
        prev = carry_ref[c]
        carry_ref[c] = up[ts - 16:ts]
        roll1 = lambda t: pltpu.roll(t, 1, axis=0)
        head1 = jnp.where(first_sublane, roll1(prev[8:16]), roll1(up[ts - 8:ts]))
        head2 = jnp.where(first_sublane, roll1(prev[0:8]), roll1(up[ts - 16:ts - 8]))
        return (jnp.concatenate([head1, up[0:ts - 8]], axis=0),
                jnp.concatenate([head2, head1, up[0:ts - 16]], axis=0))

    def neg_conv3(col0, carry_ref, c):
        cols = slice(col0 + c * FFN_CHUNK, col0 + (c + 1) * FFN_CHUNK)
        up = jnp.dot(x1pb, w_up_ref[:, cols], preferred_element_type=F32)
        up1, up2 = causal_taps(up, carry_ref, c)
        w0, w1, w2 = -fw_ref[0:1, cols], -fw_ref[1:2, cols], -fw_ref[2:3, cols]
        return w2 * up + (w1 * up1 + (w0 * up2 - fb_ref[:, cols]))

    split = DOWN_SPLIT_CHUNKS * FFN_CHUNK
    ffn = None
    for c in range(n_chunks):
        ng = neg_conv3(0, cg_carry_ref, c)
        nu = neg_conv3(d_ff, cu_carry_ref, c)
        act = ng * nu * (1.0 / (1.0 + jnp.exp(ng)))
        act_ref[:, c * FFN_CHUNK:(c + 1) * FFN_CHUNK] = act.astype(BF16)
        emit_mixer(interleavable * (c + 1) // n_chunks)
        if (c + 1) * FFN_CHUNK == split:
            ffn = jnp.dot(act_ref[:, 0:split], wd_ref[0:split, :], preferred_element_type=F32)
    ffn = ffn + jnp.dot(act_ref[:, split:], wd_ref[split:, :], preferred_element_type=F32)
    for _ in mixer_pieces:
        pass

    x1p = jnp.concatenate([perm_ref[l] for l in range(n_slabs)], axis=1)
    out_p = _layer_norm(alpha * x1p + ffn, l2g_ref[...], l2b_ref[...])

    pitch = nat_ref.shape[1] // 8
    for l in range(n_slabs):
        for v in range(seg):
            nat_ref[l, pl.ds(v, 8, stride=pitch), :] = out_p[v * 8:(v + 1) * 8, l * LANES:(l + 1) * LANES]
    for l in range(n_slabs):
        for s in range(8):
            o_ref[s * seg:(s + 1) * seg, l * LANES:(l + 1) * LANES] = nat_ref[l, s * pitch:s * pitch + seg, :]


def _full_spec(shape):
    return pl.BlockSpec(shape, lambda g: (0,) * len(shape))


def _layer(x, alpha, w_in, b_in, sinks, tbl, cw, cb, clg, clb, ag, cg, w_out, b_out, l1g, l1b,
           w_up, fw, fb, w_down, l2g, l2b, seq_tile):
    bsz, seq, d_model = x.shape
    d_conv = cw.shape[1]
    d_ff = w_down.shape[0]
    d_kv = N_KV_HEADS * HEAD_DIM
    n_chunks = d_ff // FFN_CHUNK
    tiles_per_row = seq // seq_tile
    n_tiles = bsz * tiles_per_row
    seg = seq_tile // 8
    nat_pitch = seg + 8
    assert seq % seq_tile == 0 and seq_tile % WINDOW == 0 and d_ff % FFN_CHUNK == 0
    assert seg % 16 == 0 and d_model % LANES == 0

    row = lambda v: v.reshape(1, -1).astype(F32)
    weights = [w.astype(F32) for w in (w_in, w_out, w_up, w_down)]
    assert all(w.shape[0] % STAGE_ROWS == 0 and w.shape[1] % LANES == 0 for w in weights)
    args = [
        x,
        weights[0], row(b_in), sinks.astype(F32), tbl.reshape(-1).astype(F32),
        cw.astype(F32), row(cb), row(clg), row(clb), row(ag), row(cg),
        weights[1], row(b_out), row(l1g), row(l1b),
        weights[2], fw.astype(F32), row(fb), weights[3], row(l2g), row(l2b),
    ]
    smem = pl.BlockSpec(memory_space=pltpu.SMEM)
    in_hbm = pl.BlockSpec(memory_space=pl.ANY)
    big = {1, 11, 15, 18}

    def mixer_tile(g):
        t = jnp.minimum(g, n_tiles - 1)
        return (t // tiles_per_row, t % tiles_per_row, 0)

    def ffn_tile(g):
        t = jnp.maximum(g - 1, 0)
        return (t // tiles_per_row, t % tiles_per_row, 0)

    in_specs = [pl.BlockSpec((None, seq_tile, d_model), mixer_tile)]
    for idx, a in enumerate(args[1:], start=1):
        in_specs.append(smem if idx in (3, 4) else in_hbm if idx in big else _full_spec(a.shape))

    scratch = [pltpu.VMEM(w.shape, BF16) for w in weights] + [
        pltpu.VMEM((STAGE_SLOTS, STAGE_ROWS, STAGE_COLS), F32),
        pltpu.SemaphoreType.DMA((STAGE_SLOTS,)),
        pltpu.VMEM((N_HEADS, WINDOW, WINDOW), F32),
        pltpu.VMEM((CONV_WIDTH, 8, d_conv), F32),
        pltpu.VMEM((8, seq_tile + WINDOW, d_kv), BF16),
        pltpu.VMEM((seq_tile + CONV_HALO, d_conv), F32),
        pltpu.VMEM((7, seq_tile + CONV_HALO - 8, d_conv), F32),
        pltpu.VMEM((seq_tile, d_model), BF16),
        pltpu.VMEM((seq_tile, d_model), F32),
        pltpu.VMEM((d_model // LANES, seq_tile, LANES), F32),
        pltpu.VMEM((d_model // LANES, 8 * nat_pitch, LANES), F32),
        pltpu.VMEM((seq_tile, d_ff), BF16),
        pltpu.VMEM((n_chunks, 16, FFN_CHUNK), F32),
        pltpu.VMEM((n_chunks, 16, FFN_CHUNK), F32),
    ]
    return pl.pallas_call(
        functools.partial(_layer_kernel, alpha, n_tiles, tiles_per_row),
        grid=(n_tiles + 1,),
        in_specs=in_specs,
        out_specs=pl.BlockSpec((None, seq_tile, d_model), ffn_tile),
        out_shape=jax.ShapeDtypeStruct(x.shape, x.dtype),
        scratch_shapes=scratch,
        compiler_params=pltpu.CompilerParams(
            dimension_semantics=("arbitrary",),
            vmem_limit_bytes=VMEM_LIMIT_BYTES),
        name="hybrid_layer",
    )(*args)


def kernel(x, w_in, b_in, attn_sinks, rel_bias_table, conv_dw_w, conv_dw_b, conv_ln_g, conv_ln_b,
           attn_out_gain, conv_out_gain, w_out, b_out, ln1_g, ln1_b, w_up, ffn_dw_w, ffn_dw_b, w_down,
           ln2_g, ln2_b, seq_tile=SEQ_TILE):
    depth = w_in.shape[0]
    alpha = (2.0 * depth) ** 0.25
    for l in range(depth):
        x = _layer(x, alpha, w_in[l], b_in[l], attn_sinks[l], rel_bias_table, conv_dw_w[l], conv_dw_b[l],
                   conv_ln_g[l], conv_ln_b[l], attn_out_gain[l], conv_out_gain[l], w_out[l], b_out[l],
                   ln1_g[l], ln1_b[l], w_up[l], ffn_dw_w[l], ffn_dw_b[l], w_down[l], ln2_g[l], ln2_b[l],
                   seq_tile)
    return x
```

```python
import functools
import math

import jax
import jax.numpy as jnp
from jax import lax
from jax.experimental import pallas as pl
from jax.experimental.pallas import tpu as pltpu

HEAD_DIM = 64
N_KV_HEADS = 2
GQA_GROUP = 4
N_HEADS = N_KV_HEADS * GQA_GROUP
WINDOW = 128
CONV_WIDTH = 31
N_BUCKETS = 32
MAX_DISTANCE = 128
FFN_CONV_WIDTH = 3
LN_EPS = 1e-5

SEQ_TILE = 512
CONV_HALO = 32
CONV_ROWS = 32
FFN_CHUNK = 256
DOWN_SPLIT_CHUNKS = 8
LANES = 128
W_GLU, W_SHIFT, W_KV, W_Q, W_ATTN, W_CONV = 1300, 900, 500, 400, 560, 1400
STAGE_ROWS, STAGE_COLS = 256, 1792
STAGE_SLOTS = 4
VMEM_LIMIT_BYTES = 61 * 1024 * 1024

BF16 = jnp.bfloat16
F32 = jnp.float32


def _layer_norm(r, g, b):
    mu = jnp.mean(r, axis=-1, keepdims=True)
    d = r - mu
    var = jnp.mean(d * d, axis=-1, keepdims=True)
    return d * lax.rsqrt(var + LN_EPS) * g + b


def _rms_norm(y, g):
    return y * lax.rsqrt(jnp.mean(y * y, axis=-1, keepdims=True) + LN_EPS) * g


def _dot_nt(a, b):
    return lax.dot_general(a, b, (((1,), (1,)), ((), ())), preferred_element_type=F32)


def _layer_kernel(alpha, n_tiles, tiles_per_row,
                  x_ref, w_in_hbm, b_in_ref, sinks_ref, tbl_ref, cw_ref, cb_ref, clg_ref, clb_ref,
                  ag_ref, cg_ref, w_out_hbm, b_out_ref, l1g_ref, l1b_ref,
                  w_up_hbm, fw_ref, fb_ref, wd_hbm, l2g_ref, l2b_ref,
                  o_ref,
                  w_in_ref, w_out_ref, w_up_ref, wd_ref, stage_ref, stage_sem, bias_ref, cwb_ref, kv_ref, hbuf_ref, hshift_ref, y_ref, x1_ref, perm_ref, nat_ref,
                  act_ref, cg_carry_ref, cu_carry_ref):
    ts = x_ref.shape[0]
    d_model = x_ref.shape[1]
    d_attn = N_HEADS * HEAD_DIM
    d_kv = N_KV_HEADS * HEAD_DIM
    d_conv = cw_ref.shape[1]
    d_ff = wd_ref.shape[0]
    n_chunks = d_ff // FFN_CHUNK
    n_slabs = d_model // LANES
    g = pl.program_id(0)
    i = jnp.minimum(g, n_tiles - 1) % tiles_per_row
    fi = jnp.maximum(g - 1, 0) % tiles_per_row

    qi = lax.broadcasted_iota(jnp.int32, (WINDOW, WINDOW), 0)
    kj = lax.broadcasted_iota(jnp.int32, (WINDOW, WINDOW), 1)
    tri = kj <= qi

    def load_weights(pairs):
        n_slots, stage_rows, stage_cols = stage_ref.shape
        blocks = []
        for src_hbm, dst_ref in pairs:
            rows, cols = src_hbm.shape
            width = max(w for w in range(LANES, stage_cols + 1, LANES) if cols % w == 0)
            blocks += [(src_hbm, dst_ref, r0, c0, width)
                       for c0 in range(0, cols, width) for r0 in range(0, rows, stage_rows)]

        def copy(k):
            src_hbm, _, r0, c0, width = blocks[k]
            slot = k % n_slots
            return pltpu.make_async_copy(src_hbm.at[pl.ds(r0, stage_rows), pl.ds(c0, width)],
                                         stage_ref.at[slot, :, pl.ds(0, width)], stage_sem.at[slot])

        for k in range(min(n_slots - 1, len(blocks))):
            copy(k).start()
        for k, (_, dst_ref, r0, c0, width) in enumerate(blocks):
            if k + n_slots - 1 < len(blocks):
                copy(k + n_slots - 1).start()
            copy(k).wait()
            dst_ref[r0:r0 + stage_rows, c0:c0 + width] = stage_ref[k % n_slots, :, 0:width].astype(BF16)

    @pl.when(g == 0)
    def _first_step():
        load_weights(((w_in_hbm, w_in_ref), (w_out_hbm, w_out_ref), (w_up_hbm, w_up_ref), (wd_hbm, wd_ref)))
        x1_ref[...] = jnp.zeros(x1_ref.shape, F32)
        for k in range(CONV_WIDTH):
            cwb_ref[k] = jnp.broadcast_to(cw_ref[k:k + 1, :], (8, d_conv))
        n = (qi - kj) & (WINDOW - 1)
        max_exact = N_BUCKETS // 2
        nf = jnp.maximum(n, max_exact).astype(F32)
        large = max_exact + (jnp.log(nf / max_exact) / math.log(MAX_DISTANCE / max_exact)
                             * (N_BUCKETS - max_exact)).astype(jnp.int32)
        large = jnp.minimum(large, N_BUCKETS - 1)
        bucket = jnp.where(n < max_exact, n, large)
        for hd in range(N_HEADS):
            bias = jnp.zeros((WINDOW, WINDOW), F32)
            for bkt in range(N_BUCKETS):
                bias = jnp.where(bucket == bkt, tbl_ref[bkt * N_HEADS + hd], bias)
            bias_ref[hd] = bias

    @pl.when(i == 0)
    def _reset_mixer_history():
        kv_ref[:, 0:WINDOW, :] = jnp.zeros((8, WINDOW, d_kv), BF16)
        hbuf_ref[0:CONV_HALO, :] = jnp.zeros((CONV_HALO, d_conv), F32)

    @pl.when(fi == 0)
    def _reset_ffn_history():
        cg_carry_ref[...] = jnp.zeros(cg_carry_ref.shape, F32)
        cu_carry_ref[...] = jnp.zeros(cu_carry_ref.shape, F32)

    def mixer():
        x = x_ref[...]
        proj = jnp.dot(x.astype(BF16), w_in_ref[...], preferred_element_type=F32) + b_in_ref[...]
        q_end, k_end, v_end, a_end = d_attn, d_attn + d_kv, d_attn + 2 * d_kv, d_attn + 2 * d_kv + d_conv
        yield 0

        hbuf_ref[CONV_HALO:CONV_HALO + ts, :] = proj[:, v_end:a_end] * jax.nn.sigmoid(proj[:, a_end:])
        yield W_GLU
        for s in range(1, 8):
            hshift_ref[s - 1] = hbuf_ref[s:s + hshift_ref.shape[1], :]
            yield W_SHIFT

        lane = lax.broadcasted_iota(jnp.int32, (ts, d_kv), 1)
        lo = lane < HEAD_DIM
        for base, t in ((0, proj[:, q_end:k_end]), (4, proj[:, k_end:v_end])):
            tr = pltpu.roll(t, HEAD_DIM, axis=1)
            zero = jnp.zeros_like(t)
            kv_ref[base + 0, WINDOW:WINDOW + ts, :] = jnp.where(lo, t, zero).astype(BF16)
            kv_ref[base + 1, WINDOW:WINDOW + ts, :] = jnp.where(lo, zero, tr).astype(BF16)
            kv_ref[base + 2, WINDOW:WINDOW + ts, :] = jnp.where(lo, tr, zero).astype(BF16)
            kv_ref[base + 3, WINDOW:WINDOW + ts, :] = jnp.where(lo, zero, t).astype(BF16)
            yield W_KV
        qs = (proj[:, 0:q_end] * (HEAD_DIM ** -0.5)).astype(BF16)
        yield W_Q

        conv, attn = conv_pieces(), attention_pieces(qs)
        for w in conv:
            yield w
            yield next(conv, 0)
            yield next(attn, 0)
        for w in attn:
            yield w

        mix = jnp.dot(y_ref[...], w_out_ref[...], preferred_element_type=F32) + b_out_ref[...]
        x1_ref[...] = _layer_norm(alpha * x + mix, l1g_ref[...], l1b_ref[...])
        yield 0

    def attention_pieces(qs):
        lo_w = lax.broadcasted_iota(jnp.int32, (WINDOW, 2 * HEAD_DIM), 1) < HEAD_DIM
        units = [(j, h) for j in range(ts // WINDOW) for h in range(N_KV_HEADS)]
        scores, probs, inv, pair_out = {}, {}, {}, {}

        def score_dots(j, h):
            qb = qs[j * WINDOW:(j + 1) * WINDOW, :]
            q2 = jnp.concatenate([qb[:, (2 * h) * LANES:(2 * h + 1) * LANES],
                                  qb[:, (2 * h + 1) * LANES:(2 * h + 2) * LANES]], axis=0)
            scores[j, h] = [_dot_nt(q2, kv_ref[2 * h + ab, j * WINDOW:(j + 2) * WINDOW, :])
                            for ab in range(2)]

        def softmax(j, h):
            has_prev = (i > 0) | (j > 0)
            prev_mask = jnp.where(has_prev, 0.0, -jnp.inf).astype(F32)
            for ab in range(2):
                s = scores[j, h][ab]
                for pp in range(2):
                    hd = 4 * h + 2 * pp + ab
                    s_prev = s[pp * WINDOW:(pp + 1) * WINDOW, 0:WINDOW]
                    s_cur = s[pp * WINDOW:(pp + 1) * WINDOW, WINDOW:2 * WINDOW]
                    sc = jnp.where(tri, s_cur, s_prev + prev_mask) + bias_ref[hd]
                    sink = sinks_ref[hd]
                    m = jnp.maximum(jnp.max(sc, axis=-1, keepdims=True), sink)
                    p = jnp.exp(sc - m)
                    denom = jnp.sum(p, axis=-1, keepdims=True) + jnp.exp(sink - m)
                    zero = jnp.zeros_like(p)
                    probs[j, h, pp, ab] = jnp.concatenate(
                        [jnp.where(tri, zero, p), jnp.where(tri, p, zero)], axis=1).astype(BF16)
                    inv[j, h, pp, ab] = 1.0 / denom

        def weighted_values(j, h):
            rows = slice(j * WINDOW, (j + 2) * WINDOW)
            for pp in range(2):
                o = (jnp.dot(probs[j, h, pp, 0], kv_ref[4 + 2 * h + 0, rows, :], preferred_element_type=F32)
                     + jnp.dot(probs[j, h, pp, 1], kv_ref[4 + 2 * h + 1, rows, :], preferred_element_type=F32))
                pair_out[j, h, pp] = o * jnp.where(lo_w, inv[j, h, pp, 0], inv[j, h, pp, 1])
            if h + 1 == N_KV_HEADS:
                ya = jnp.concatenate([pair_out[j, hh, pp] for hh in range(N_KV_HEADS) for pp in range(2)], axis=1)
                y_ref[j * WINDOW:(j + 1) * WINDOW, 0:d_attn] = _rms_norm(ya, ag_ref[...]).astype(BF16)

        for t in range(len(units) + 2):
            if t >= 2:
                weighted_values(*units[t - 2])
            if 1 <= t <= len(units):
                softmax(*units[t - 1])
            if t < len(units):
                score_dots(*units[t])
            yield W_ATTN
        kv_ref[:, 0:WINDOW, :] = kv_ref[:, ts:ts + WINDOW, :]

    def conv_pieces():
        for c in range(ts // CONV_ROWS):
            r0 = c * CONV_ROWS
            acc = jnp.broadcast_to(cb_ref[...], (CONV_ROWS, d_conv))
            for k in range(CONV_WIDTH):
                off = CONV_HALO - (CONV_WIDTH - 1) + k
                s, aligned = off % 8, off - off % 8
                rows = slice(r0 + aligned, r0 + aligned + CONV_ROWS)
                tap = hbuf_ref[rows, :] if s == 0 else hshift_ref[s - 1, rows, :]
                acc = acc + jnp.concatenate([cwb_ref[k]] * (CONV_ROWS // 8), axis=0) * tap
            hn = _layer_norm(acc, clg_ref[...], clb_ref[...])
            y_ref[r0:r0 + CONV_ROWS, d_attn:d_attn + d_conv] = (
                _rms_norm(hn * jax.nn.sigmoid(hn), cg_ref[...]).astype(BF16))
            yield W_CONV
        hbuf_ref[0:CONV_HALO, :] = hbuf_ref[ts:ts + CONV_HALO, :]

    mixer_pieces = mixer()
    interleavable = (W_GLU + 7 * W_SHIFT + 2 * W_KV + W_Q + ((ts // WINDOW) * N_KV_HEADS + 2) * W_ATTN
                     + (ts // CONV_ROWS) * W_CONV)

    emitted = [0]

    def emit_mixer(target):
        while emitted[0] < target:
            emitted[0] += next(mixer_pieces)

    seg = ts // 8
    x1_prev = x1_ref[...]
    for l in range(n_slabs):
        for s in range(8):
            for u in range(seg // 8):
                n0 = s * seg + 8 * u
                perm_ref[l, pl.ds(64 * u + s, 8, stride=8), :] = x1_prev[n0:n0 + 8, l * LANES:(l + 1) * LANES]
    x1pb = jnp.concatenate([perm_ref[l] for l in range(n_slabs)], axis=1).astype(BF16)
    next(mixer_pieces)

    first_sublane = lax.broadcasted_iota(jnp.int32, (8, FFN_CHUNK), 0) == 0

    def causal_taps(up, carry_ref, c):
        prev = carry_ref[c]
        carry_ref[c] = up[ts - 16:ts]
        roll1 = lambda t: pltpu.roll(t, 1, axis=0)
        head1 = jnp.where(first_sublane, roll1(prev[8:16]), roll1(up[ts - 8:ts]))
        head2 = jnp.where(first_sublane, roll1(prev[0:8]), roll1(up[ts - 16:ts - 8]))
        return (jnp.concatenate([head1, up[0:ts - 8]], axis=0),
                jnp.concatenate([head2, head1, up[0:ts - 16]], axis=0))

    def neg_conv3(col0, carry_ref, c):
        cols = slice(col0 + c * FFN_CHUNK, col0 + (c + 1) * FFN_CHUNK)
        up = jnp.dot(x1pb, w_up_ref[:, cols], preferred_element_type=F32)
        up1, up2 = causal_taps(up, carry_ref, c)
        w0, w1, w2 = -fw_ref[0:1, cols], -fw_ref[1:2, cols], -fw_ref[2:3, cols]
        return w2 * up + (w1 * up1 + (w0 * up2 - fb_ref[:, cols]))

    split = DOWN_SPLIT_CHUNKS * FFN_CHUNK
    ffn = None
    for c in range(n_chunks):
        ng = neg_conv3(0, cg_carry_ref, c)
        nu = neg_conv3(d_ff, cu_carry_ref, c)
        act = ng * nu * (1.0 / (1.0 + jnp.exp(ng)))
        act_ref[:, c * FFN_CHUNK:(c + 1) * FFN_CHUNK] = act.astype(BF16)
        emit_mixer(interleavable * (c + 1) // n_chunks)
        if (c + 1) * FFN_CHUNK == split:
            ffn = jnp.dot(act_ref[:, 0:split], wd_ref[0:split, :], preferred_element_type=F32)
    ffn = ffn + jnp.dot(act_ref[:, split:], wd_ref[split:, :], preferred_element_type=F32)
    for _ in mixer_pieces:
        pass

    x1p = jnp.concatenate([perm_ref[l] for l in range(n_slabs)], axis=1)
    out_p = _layer_norm(alpha * x1p + ffn, l2g_ref[...], l2b_ref[...])

    pitch = nat_ref.shape[1] // 8
    for l in range(n_slabs):
        for v in range(seg):
            nat_ref[l, pl.ds(v, 8, stride=pitch), :] = out_p[v * 8:(v + 1) * 8, l * LANES:(l + 1) * LANES]
    for l in range(n_slabs):
        for s in range(8):
            o_ref[s * seg:(s + 1) * seg, l * LANES:(l + 1) * LANES] = nat_ref[l, s * pitch:s * pitch + seg, :]


def _full_spec(shape):
    return pl.BlockSpec(shape, lambda g: (0,) * len(shape))


def _layer(x, alpha, w_in, b_in, sinks, tbl, cw, cb, clg, clb, ag, cg, w_out, b_out, l1g, l1b,
           w_up, fw, fb, w_down, l2g, l2b, seq_tile):
    bsz, seq, d_model = x.shape
    d_conv = cw.shape[1]
    d_ff = w_down.shape[0]
    d_kv = N_KV_HEADS * HEAD_DIM
    n_chunks = d_ff // FFN_CHUNK
    tiles_per_row = seq // seq_tile
    n_tiles = bsz * tiles_per_row
    seg = seq_tile // 8
    nat_pitch = seg + 8
    assert seq % seq_tile == 0 and seq_tile % WINDOW == 0 and d_ff % FFN_CHUNK == 0
    assert seg % 16 == 0 and d_model % LANES == 0

    row = lambda v: v.reshape(1, -1).astype(F32)
    weights = [w.astype(F32) for w in (w_in, w_out, w_up, w_down)]
    assert all(w.shape[0] % STAGE_ROWS == 0 and w.shape[1] % LANES == 0 for w in weights)
    args = [
        x,
        weights[0], row(b_in), sinks.astype(F32), tbl.reshape(-1).astype(F32),
        cw.astype(F32), row(cb), row(clg), row(clb), row(ag), row(cg),
        weights[1], row(b_out), row(l1g), row(l1b),
        weights[2], fw.astype(F32), row(fb), weights[3], row(l2g), row(l2b),
    ]
    smem = pl.BlockSpec(memory_space=pltpu.SMEM)
    in_hbm = pl.BlockSpec(memory_space=pl.ANY)
    big = {1, 11, 15, 18}

    def mixer_tile(g):
        t = jnp.minimum(g, n_tiles - 1)
        return (t // tiles_per_row, t % tiles_per_row, 0)

    def ffn_tile(g):
        t = jnp.maximum(g - 1, 0)
        return (t // tiles_per_row, t % tiles_per_row, 0)

    in_specs = [pl.BlockSpec((None, seq_tile, d_model), mixer_tile)]
    for idx, a in enumerate(args[1:], start=1):
        in_specs.append(smem if idx in (3, 4) else in_hbm if idx in big else _full_spec(a.shape))

    scratch = [pltpu.VMEM(w.shape, BF16) for w in weights] + [
        pltpu.VMEM((STAGE_SLOTS, STAGE_ROWS, STAGE_COLS), F32),
        pltpu.SemaphoreType.DMA((STAGE_SLOTS,)),
        pltpu.VMEM((N_HEADS, WINDOW, WINDOW), F32),
        pltpu.VMEM((CONV_WIDTH, 8, d_conv), F32),
        pltpu.VMEM((8, seq_tile + WINDOW, d_kv), BF16),
        pltpu.VMEM((seq_tile + CONV_HALO, d_conv), F32),
        pltpu.VMEM((7, seq_tile + CONV_HALO - 8, d_conv), F32),
        pltpu.VMEM((seq_tile, d_model), BF16),
        pltpu.VMEM((seq_tile, d_model), F32),
        pltpu.VMEM((d_model // LANES, seq_tile, LANES), F32),
        pltpu.VMEM((d_model // LANES, 8 * nat_pitch, LANES), F32),
        pltpu.VMEM((seq_tile, d_ff), BF16),
        pltpu.VMEM((n_chunks, 16, FFN_CHUNK), F32),
        pltpu.VMEM((n_chunks, 16, FFN_CHUNK), F32),
    ]
    return pl.pallas_call(
        functools.partial(_layer_kernel, alpha, n_tiles, tiles_per_row),
        grid=(n_tiles + 1,),
        in_specs=in_specs,
        out_specs=pl.BlockSpec((None, seq_tile, d_model), ffn_tile),
        out_shape=jax.ShapeDtypeStruct(x.shape, x.dtype),
        scratch_shapes=scratch,
        compiler_params=pltpu.CompilerParams(
            dimension_semantics=("arbitrary",),
            vmem_limit_bytes=VMEM_LIMIT_BYTES),
        name="hybrid_layer",
    )(*args)


def kernel(x, w_in, b_in, attn_sinks, rel_bias_table, conv_dw_w, conv_dw_b, conv_ln_g, conv_ln_b,
           attn_out_gain, conv_out_gain, w_out, b_out, ln1_g, ln1_b, w_up, ffn_dw_w, ffn_dw_b, w_down,
           ln2_g, ln2_b, seq_tile=SEQ_TILE):
    depth = w_in.shape[0]
    alpha = (2.0 * depth) ** 0.25
    for l in range(depth):
        x = _layer(x, alpha, w_in[l], b_in[l], attn_sinks[l], rel_bias_table, conv_dw_w[l], conv_dw_b[l],
                   conv_ln_g[l], conv_ln_b[l], attn_out_gain[l], conv_out_gain[l], w_out[l], b_out[l],
                   ln1_g[l], ln1_b[l], w_up[l], ffn_dw_w[l], ffn_dw_b[l], w_down[l], ln2_g[l], ln2_b[l],
                   seq_tile)
    return x
```

```python
import functools
import math

import jax
import jax.numpy as jnp
from jax import lax
from jax.experimental import pallas as pl
from jax.experimental.pallas import tpu as pltpu

HEAD_DIM = 64
N_KV_HEADS = 2
GQA_GROUP = 4
N_HEADS = N_KV_HEADS * GQA_GROUP
WINDOW = 128
CONV_WIDTH = 31
N_BUCKETS = 32
MAX_DISTANCE = 128
FFN_CONV_WIDTH = 3
LN_EPS = 1e-5

SEQ_TILE = 512
CONV_HALO = 32
CONV_ROWS = 32
FFN_CHUNK = 256
DOWN_SPLIT_CHUNKS = 8
LANES = 128
W_GLU, W_SHIFT, W_KV, W_Q, W_ATTN, W_CONV = 1300, 900, 500, 400, 560, 1400
STAGE_ROWS, STAGE_COLS = 256, 1792
STAGE_SLOTS = 4
VMEM_LIMIT_BYTES = 61 * 1024 * 1024

BF16 = jnp.bfloat16
F32 = jnp.float32


def _layer_norm(r, g, b):
    mu = jnp.mean(r, axis=-1, keepdims=True)
    d = r - mu
    var = jnp.mean(d * d, axis=-1, keepdims=True)
    return d * lax.rsqrt(var + LN_EPS) * g + b


def _rms_norm(y, g):
    return y * lax.rsqrt(jnp.mean(y * y, axis=-1, keepdims=True) + LN_EPS) * g


def _dot_nt(a, b):
    return lax.dot_general(a, b, (((1,), (1,)), ((), ())), preferred_element_type=F32)


def _layer_kernel(alpha, n_tiles, tiles_per_row,
                  x_ref, w_in_hbm, b_in_ref, sinks_ref, tbl_ref, cw_ref, cb_ref, clg_ref, clb_ref,
                  ag_ref, cg_ref, w_out_hbm, b_out_ref, l1g_ref, l1b_ref,
                  w_up_hbm, fw_ref, fb_ref, wd_hbm, l2g_ref, l2b_ref,
                  o_ref,
                  w_in_ref, w_out_ref, w_up_ref, wd_ref, stage_ref, stage_sem, bias_ref, cwb_ref, kv_ref, hbuf_ref, hshift_ref, y_ref, x1_ref, perm_ref, nat_ref,
                  act_ref, cg_carry_ref, cu_carry_ref):
    ts = x_ref.shape[0]
    d_model = x_ref.shape[1]
    d_attn = N_HEADS * HEAD_DIM
    d_kv = N_KV_HEADS * HEAD_DIM
    d_conv = cw_ref.shape[1]
    d_ff = wd_ref.shape[0]
    n_chunks = d_ff // FFN_CHUNK
    n_slabs = d_model // LANES
    g = pl.program_id(0)
    i = jnp.minimum(g, n_tiles - 1) % tiles_per_row
    fi = jnp.maximum(g - 1, 0) % tiles_per_row

    qi = lax.broadcasted_iota(jnp.int32, (WINDOW, WINDOW), 0)
    kj = lax.broadcasted_iota(jnp.int32, (WINDOW, WINDOW), 1)
    tri = kj <= qi

    def load_weights(pairs):
        n_slots, stage_rows, stage_cols = stage_ref.shape
        blocks = []
        for src_hbm, dst_ref in pairs:
            rows, cols = src_hbm.shape
            width = max(w for w in range(LANES, stage_cols + 1, LANES) if cols % w == 0)
            blocks += [(src_hbm, dst_ref, r0, c0, width)
                       for c0 in range(0, cols, width) for r0 in range(0, rows, stage_rows)]

        def copy(k):
            src_hbm, _, r0, c0, width = blocks[k]
            slot = k % n_slots
            return pltpu.make_async_copy(src_hbm.at[pl.ds(r0, stage_rows), pl.ds(c0, width)],
                                         stage_ref.at[slot, :, pl.ds(0, width)], stage_sem.at[slot])

        for k in range(min(n_slots - 1, len(blocks))):
            copy(k).start(priority=k % 2)
        for k, (_, dst_ref, r0, c0, width) in enumerate(blocks):
            if k + n_slots - 1 < len(blocks):
                nxt = k + n_slots - 1
                copy(nxt).start(priority=nxt % 2)
            copy(k).wait()
            dst_ref[r0:r0 + stage_rows, c0:c0 + width] = stage_ref[k % n_slots, :, 0:width].astype(BF16)

    @pl.when(g == 0)
    def _first_step():
        load_weights(((w_in_hbm, w_in_ref), (w_out_hbm, w_out_ref), (w_up_hbm, w_up_ref), (wd_hbm, wd_ref)))
        x1_ref[...] = jnp.zeros(x1_ref.shape, F32)
        for k in range(CONV_WIDTH):
            cwb_ref[k] = jnp.broadcast_to(cw_ref[k:k + 1, :], (8, d_conv))
        n = (qi - kj) & (WINDOW - 1)
        max_exact = N_BUCKETS // 2
        nf = jnp.maximum(n, max_exact).astype(F32)
        large = max_exact + (jnp.log(nf / max_exact) / math.log(MAX_DISTANCE / max_exact)
                             * (N_BUCKETS - max_exact)).astype(jnp.int32)
        large = jnp.minimum(large, N_BUCKETS - 1)
        bucket = jnp.where(n < max_exact, n, large)
        for hd in range(N_HEADS):
            bias = jnp.zeros((WINDOW, WINDOW), F32)
            for bkt in range(N_BUCKETS):
                bias = jnp.where(bucket == bkt, tbl_ref[bkt * N_HEADS + hd], bias)
            bias_ref[hd] = bias

    @pl.when(i == 0)
    def _reset_mixer_history():
        kv_ref[:, 0:WINDOW, :] = jnp.zeros((8, WINDOW, d_kv), BF16)
        hbuf_ref[0:CONV_HALO, :] = jnp.zeros((CONV_HALO, d_conv), F32)

    @pl.when(fi == 0)
    def _reset_ffn_history():
        cg_carry_ref[...] = jnp.zeros(cg_carry_ref.shape, F32)
        cu_carry_ref[...] = jnp.zeros(cu_carry_ref.shape, F32)

    def mixer():
        x = x_ref[...]
        proj = jnp.dot(x.astype(BF16), w_in_ref[...], preferred_element_type=F32) + b_in_ref[...]
        q_end, k_end, v_end, a_end = d_attn, d_attn + d_kv, d_attn + 2 * d_kv, d_attn + 2 * d_kv + d_conv
        yield 0

        hbuf_ref[CONV_HALO:CONV_HALO + ts, :] = proj[:, v_end:a_end] * jax.nn.sigmoid(proj[:, a_end:])
        yield W_GLU
        for s in range(1, 8):
            hshift_ref[s - 1] = hbuf_ref[s:s + hshift_ref.shape[1], :]
            yield W_SHIFT

        lane = lax.broadcasted_iota(jnp.int32, (ts, d_kv), 1)
        lo = lane < HEAD_DIM
        for base, t in ((0, proj[:, q_end:k_end]), (4, proj[:, k_end:v_end])):
            tr = pltpu.roll(t, HEAD_DIM, axis=1)
            zero = jnp.zeros_like(t)
            kv_ref[base + 0, WINDOW:WINDOW + ts, :] = jnp.where(lo, t, zero).astype(BF16)
            kv_ref[base + 1, WINDOW:WINDOW + ts, :] = jnp.where(lo, zero, tr).astype(BF16)
            kv_ref[base + 2, WINDOW:WINDOW + ts, :] = jnp.where(lo, tr, zero).astype(BF16)
            kv_ref[base + 3, WINDOW:WINDOW + ts, :] = jnp.where(lo, zero, t).astype(BF16)
            yield W_KV
        qs = (proj[:, 0:q_end] * (HEAD_DIM ** -0.5)).astype(BF16)
        yield W_Q

        conv, attn = conv_pieces(), attention_pieces(qs)
        for w in conv:
            yield w
            yield next(conv, 0)
            yield next(attn, 0)
        for w in attn:
            yield w

        mix = jnp.dot(y_ref[...], w_out_ref[...], preferred_element_type=F32) + b_out_ref[...]
        x1_ref[...] = _layer_norm(alpha * x + mix, l1g_ref[...], l1b_ref[...])
        yield 0

    def attention_pieces(qs):
        lo_w = lax.broadcasted_iota(jnp.int32, (WINDOW, 2 * HEAD_DIM), 1) < HEAD_DIM
        units = [(j, h) for j in range(ts // WINDOW) for h in range(N_KV_HEADS)]
        scores, probs, inv, pair_out = {}, {}, {}, {}

        def score_dots(j, h):
            qb = qs[j * WINDOW:(j + 1) * WINDOW, :]
            q2 = jnp.concatenate([qb[:, (2 * h) * LANES:(2 * h + 1) * LANES],
                                  qb[:, (2 * h + 1) * LANES:(2 * h + 2) * LANES]], axis=0)
            scores[j, h] = [_dot_nt(q2, kv_ref[2 * h + ab, j * WINDOW:(j + 2) * WINDOW, :])
                            for ab in range(2)]

        def softmax(j, h):
            has_prev = (i > 0) | (j > 0)
            prev_mask = jnp.where(has_prev, 0.0, -jnp.inf).astype(F32)
            for ab in range(2):
                s = scores[j, h][ab]
                for pp in range(2):
                    hd = 4 * h + 2 * pp + ab
                    s_prev = s[pp * WINDOW:(pp + 1) * WINDOW, 0:WINDOW]
                    s_cur = s[pp * WINDOW:(pp + 1) * WINDOW, WINDOW:2 * WINDOW]
                    sc = jnp.where(tri, s_cur, s_prev + prev_mask) + bias_ref[hd]
                    sink = sinks_ref[hd]
                    m = jnp.maximum(jnp.max(sc, axis=-1, keepdims=True), sink)
                    p = jnp.exp(sc - m)
                    denom = jnp.sum(p, axis=-1, keepdims=True) + jnp.exp(sink - m)
                    zero = jnp.zeros_like(p)
                    probs[j, h, pp, ab] = jnp.concatenate(
                        [jnp.where(tri, zero, p), jnp.where(tri, p, zero)], axis=1).astype(BF16)
                    inv[j, h, pp, ab] = 1.0 / denom

        def weighted_values(j, h):
            rows = slice(j * WINDOW, (j + 2) * WINDOW)
            for pp in range(2):
                o = (jnp.dot(probs[j, h, pp, 0], kv_ref[4 + 2 * h + 0, rows, :], preferred_element_type=F32)
                     + jnp.dot(probs[j, h, pp, 1], kv_ref[4 + 2 * h + 1, rows, :], preferred_element_type=F32))
                pair_out[j, h, pp] = o * jnp.where(lo_w, inv[j, h, pp, 0], inv[j, h, pp, 1])
            if h + 1 == N_KV_HEADS:
                ya = jnp.concatenate([pair_out[j, hh, pp] for hh in range(N_KV_HEADS) for pp in range(2)], axis=1)
                y_ref[j * WINDOW:(j + 1) * WINDOW, 0:d_attn] = _rms_norm(ya, ag_ref[...]).astype(BF16)

        for t in range(len(units) + 2):
            if t >= 2:
                weighted_values(*units[t - 2])
            if 1 <= t <= len(units):
                softmax(*units[t - 1])
            if t < len(units):
                score_dots(*units[t])
            yield W_ATTN
        kv_ref[:, 0:WINDOW, :] = kv_ref[:, ts:ts + WINDOW, :]

    def conv_pieces():
        for c in range(ts // CONV_ROWS):
            r0 = c * CONV_ROWS
            acc = jnp.broadcast_to(cb_ref[...], (CONV_ROWS, d_conv))
            for k in range(CONV_WIDTH):
                off = CONV_HALO - (CONV_WIDTH - 1) + k
                s, aligned = off % 8, off - off % 8
                rows = slice(r0 + aligned, r0 + aligned + CONV_ROWS)
                tap = hbuf_ref[rows, :] if s == 0 else hshift_ref[s - 1, rows, :]
                acc = acc + jnp.concatenate([cwb_ref[k]] * (CONV_ROWS // 8), axis=0) * tap
            hn = _layer_norm(acc, clg_ref[...], clb_ref[...])
            y_ref[r0:r0 + CONV_ROWS, d_attn:d_attn + d_conv] = (
                _rms_norm(hn * jax.nn.sigmoid(hn), cg_ref[...]).astype(BF16))
            yield W_CONV
        hbuf_ref[0:CONV_HALO, :] = hbuf_ref[ts:ts + CONV_HALO, :]

    mixer_pieces = mixer()
    interleavable = (W_GLU + 7 * W_SHIFT + 2 * W_KV + W_Q + ((ts // WINDOW) * N_KV_HEADS + 2) * W_ATTN
                     + (ts // CONV_ROWS) * W_CONV)

    emitted = [0]

    def emit_mixer(target):
        while emitted[0] < target:
            emitted[0] += next(mixer_pieces)

    seg = ts // 8
    x1_prev = x1_ref[...]
    for l in range(n_slabs):
        for s in range(8):
            for u in range(seg // 8):
                n0 = s * seg + 8 * u
                perm_ref[l, pl.ds(64 * u + s, 8, stride=8), :] = x1_prev[n0:n0 + 8, l * LANES:(l + 1) * LANES]
    x1pb = jnp.concatenate([perm_ref[l] for l in range(n_slabs)], axis=1).astype(BF16)
    next(mixer_pieces)

    first_sublane = lax.broadcasted_iota(jnp.int32, (8, FFN_CHUNK), 0) == 0

    def causal_taps(up, carry_ref, c):
        prev = carry_ref[c]
        carry_ref[c] = up[ts - 16:ts]
        roll1 = lambda t: pltpu.roll(t, 1, axis=0)
        head1 = jnp.where(first_sublane, roll1(prev[8:16]), roll1(up[ts - 8:ts]))
        head2 = jnp.where(first_sublane, roll1(prev[0:8]), roll1(up[ts - 16:ts - 8]))
        return (jnp.concatenate([head1, up[0:ts - 8]], axis=0),
                jnp.concatenate([head2, head1, up[0:ts - 16]], axis=0))

    def neg_conv3(col0, carry_ref, c):
        cols = slice(col0 + c * FFN_CHUNK, col0 + (c + 1) * FFN_CHUNK)
        up = jnp.dot(x1pb, w_up_ref[:, cols], preferred_element_type=F32)
        up1, up2 = causal_taps(up, carry_ref, c)
        w0, w1, w2 = -fw_ref[0:1, cols], -fw_ref[1:2, cols], -fw_ref[2:3, cols]
        return w2 * up + (w1 * up1 + (w0 * up2 - fb_ref[:, cols]))

    split = DOWN_SPLIT_CHUNKS * FFN_CHUNK
    ffn = None
    for c in range(n_chunks):
        ng = neg_conv3(0, cg_carry_ref, c)
        nu = neg_conv3(d_ff, cu_carry_ref, c)
        act = ng * nu * (1.0 / (1.0 + jnp.exp(ng)))
        act_ref[:, c * FFN_CHUNK:(c + 1) * FFN_CHUNK] = act.astype(BF16)
        emit_mixer(interleavable * (c + 1) // n_chunks)
        if (c + 1) * FFN_CHUNK == split:
            ffn = jnp.dot(act_ref[:, 0:split], wd_ref[0:split, :], preferred_element_type=F32)
    ffn = ffn + jnp.dot(act_ref[:, split:], wd_ref[split:, :], preferred_element_type=F32)
    for _ in mixer_pieces:
        pass

    x1p = jnp.concatenate([perm_ref[l] for l in range(n_slabs)], axis=1)
    out_p = _layer_norm(alpha * x1p + ffn, l2g_ref[...], l2b_ref[...])

    pitch = nat_ref.shape[1] // 8
    for l in range(n_slabs):
        for v in range(seg):
            nat_ref[l, pl.ds(v, 8, stride=pitch), :] = out_p[v * 8:(v + 1) * 8, l * LANES:(l + 1) * LANES]
    for l in range(n_slabs):
        for s in range(8):
            o_ref[s * seg:(s + 1) * seg, l * LANES:(l + 1) * LANES] = nat_ref[l, s * pitch:s * pitch + seg, :]


def _full_spec(shape):
    return pl.BlockSpec(shape, lambda g: (0,) * len(shape))


def _layer(x, alpha, w_in, b_in, sinks, tbl, cw, cb, clg, clb, ag, cg, w_out, b_out, l1g, l1b,
           w_up, fw, fb, w_down, l2g, l2b, seq_tile):
    bsz, seq, d_model = x.shape
    d_conv = cw.shape[1]
    d_ff = w_down.shape[0]
    d_kv = N_KV_HEADS * HEAD_DIM
    n_chunks = d_ff // FFN_CHUNK
    tiles_per_row = seq // seq_tile
    n_tiles = bsz * tiles_per_row
    seg = seq_tile // 8
    nat_pitch = seg + 8
    assert seq % seq_tile == 0 and seq_tile % WINDOW == 0 and d_ff % FFN_CHUNK == 0
    assert seg % 16 == 0 and d_model % LANES == 0

    row = lambda v: v.reshape(1, -1).astype(F32)
    weights = [w.astype(F32) for w in (w_in, w_out, w_up, w_down)]
    assert all(w.shape[0] % STAGE_ROWS == 0 and w.shape[1] % LANES == 0 for w in weights)
    args = [
        x,
        weights[0], row(b_in), sinks.astype(F32), tbl.reshape(-1).astype(F32),
        cw.astype(F32), row(cb), row(clg), row(clb), row(ag), row(cg),
        weights[1], row(b_out), row(l1g), row(l1b),
        weights[2], fw.astype(F32), row(fb), weights[3], row(l2g), row(l2b),
    ]
    smem = pl.BlockSpec(memory_space=pltpu.SMEM)
    in_hbm = pl.BlockSpec(memory_space=pl.ANY)
    big = {1, 11, 15, 18}

    def mixer_tile(g):
        t = jnp.minimum(g, n_tiles - 1)
        return (t // tiles_per_row, t % tiles_per_row, 0)

    def ffn_tile(g):
        t = jnp.maximum(g - 1, 0)
        return (t // tiles_per_row, t % tiles_per_row, 0)

    in_specs = [pl.BlockSpec((None, seq_tile, d_model), mixer_tile)]
    for idx, a in enumerate(args[1:], start=1):
        in_specs.append(smem if idx in (3, 4) else in_hbm if idx in big else _full_spec(a.shape))

    scratch = [pltpu.VMEM(w.shape, BF16) for w in weights] + [
        pltpu.VMEM((STAGE_SLOTS, STAGE_ROWS, STAGE_COLS), F32),
        pltpu.SemaphoreType.DMA((STAGE_SLOTS,)),
        pltpu.VMEM((N_HEADS, WINDOW, WINDOW), F32),
        pltpu.VMEM((CONV_WIDTH, 8, d_conv), F32),
        pltpu.VMEM((8, seq_tile + WINDOW, d_kv), BF16),
        pltpu.VMEM((seq_tile + CONV_HALO, d_conv), F32),
        pltpu.VMEM((7, seq_tile + CONV_HALO - 8, d_conv), F32),
        pltpu.VMEM((seq_tile, d_model), BF16),
        pltpu.VMEM((seq_tile, d_model), F32),
        pltpu.VMEM((d_model // LANES, seq_tile, LANES), F32),
        pltpu.VMEM((d_model // LANES, 8 * nat_pitch, LANES), F32),
        pltpu.VMEM((seq_tile, d_ff), BF16),
        pltpu.VMEM((n_chunks, 16, FFN_CHUNK), F32),
        pltpu.VMEM((n_chunks, 16, FFN_CHUNK), F32),
    ]
    return pl.pallas_call(
        functools.partial(_layer_kernel, alpha, n_tiles, tiles_per_row),
        grid=(n_tiles + 1,),
        in_specs=in_specs,
        out_specs=pl.BlockSpec((None, seq_tile, d_model), ffn_tile),
        out_shape=jax.ShapeDtypeStruct(x.shape, x.dtype),
        scratch_shapes=scratch,
        compiler_params=pltpu.CompilerParams(
            dimension_semantics=("arbitrary",),
            vmem_limit_bytes=VMEM_LIMIT_BYTES),
        name="hybrid_layer",
    )(*args)


def kernel(x, w_in, b_in, attn_sinks, rel_bias_table, conv_dw_w, conv_dw_b, conv_ln_g, conv_ln_b,
           attn_out_gain, conv_out_gain, w_out, b_out, ln1_g, ln1_b, w_up, ffn_dw_w, ffn_dw_b, w_down,
           ln2_g, ln2_b, seq_tile=SEQ_TILE):
    depth = w_in.shape[0]
    alpha = (2.0 * depth) ** 0.25
    for l in range(depth):
        x = _layer(x, alpha, w_in[l], b_in[l], attn_sinks[l], rel_bias_table, conv_dw_w[l], conv_dw_b[l],
                   conv_ln_g[l], conv_ln_b[l], attn_out_gain[l], conv_out_gain[l], w_out[l], b_out[l],
                   ln1_g[l], ln1_b[l], w_up[l], ffn_dw_w[l], ffn_dw_b[l], w_down[l], ln2_g[l], ln2_b[l],
                   seq_tile)
    return x
```

```python
import functools
import math

import jax
import jax.numpy as jnp
from jax import lax
from jax.experimental import pallas as pl
from jax.experimental.pallas import tpu as pltpu

HEAD_DIM = 64
N_KV_HEADS = 2
GQA_GROUP = 4
N_HEADS = N_KV_HEADS * GQA_GROUP
WINDOW = 128
CONV_WIDTH = 31
N_BUCKETS = 32
MAX_DISTANCE = 128
FFN_CONV_WIDTH = 3
LN_EPS = 1e-5

SEQ_TILE = 512
CONV_HALO = 32
CONV_ROWS = 32
FFN_CHUNK = 256
DOWN_SPLIT_CHUNKS = 8
LANES = 128
W_GLU, W_SHIFT, W_KV, W_Q, W_ATTN, W_CONV = 1300, 900, 500, 400, 560, 1400
STAGE_ROWS, STAGE_COLS = 256, 1792
STAGE_SLOTS = 4
VMEM_LIMIT_BYTES = 61 * 1024 * 1024

BF16 = jnp.bfloat16
F32 = jnp.float32


def _layer_norm(r, g, b):
    mu = jnp.mean(r, axis=-1, keepdims=True)
    d = r - mu
    var = jnp.mean(d * d, axis=-1, keepdims=True)
    return d * lax.rsqrt(var + LN_EPS) * g + b


def _rms_norm(y, g):
    return y * lax.rsqrt(jnp.mean(y * y, axis=-1, keepdims=True) + LN_EPS) * g


def _dot_nt(a, b):
    return lax.dot_general(a, b, (((1,), (1,)), ((), ())), preferred_element_type=F32)


def _layer_kernel(alpha, n_tiles, tiles_per_row,
                  x_ref, w_in_hbm, b_in_ref, sinks_ref, tbl_ref, cw_ref, cb_ref, clg_ref, clb_ref,
                  ag_ref, cg_ref, w_out_hbm, b_out_ref, l1g_ref, l1b_ref,
                  w_up_hbm, fw_ref, fb_ref, wd_hbm, l2g_ref, l2b_ref,
                  o_ref,
                  w_in_ref, w_out_ref, w_up_ref, wd_ref, stage_ref, stage_sem, bias_ref, cwb_ref, kv_ref, hbuf_ref, hshift_ref, y_ref, x1_ref, perm_ref, nat_ref,
                  act_ref, cg_carry_ref, cu_carry_ref):
    ts = x_ref.shape[0]
    d_model = x_ref.shape[1]
    d_attn = N_HEADS * HEAD_DIM
    d_kv = N_KV_HEADS * HEAD_DIM
    d_conv = cw_ref.shape[1]
    d_ff = wd_ref.shape[0]
    n_chunks = d_ff // FFN_CHUNK
    n_slabs = d_model // LANES
    g = pl.program_id(0)
    i = jnp.minimum(g, n_tiles - 1) % tiles_per_row
    fi = jnp.maximum(g - 1, 0) % tiles_per_row

    qi = lax.broadcasted_iota(jnp.int32, (WINDOW, WINDOW), 0)
    kj = lax.broadcasted_iota(jnp.int32, (WINDOW, WINDOW), 1)
    tri = kj <= qi

    def load_weights(pairs, while_in_flight):
        n_slots, stage_rows, stage_cols = stage_ref.shape
        blocks = []
        for src_hbm, dst_ref in pairs:
            rows, cols = src_hbm.shape
            width = max(w for w in range(LANES, stage_cols + 1, LANES) if cols % w == 0)
            blocks += [(src_hbm, dst_ref, r0, c0, width)
                       for c0 in range(0, cols, width) for r0 in range(0, rows, stage_rows)]

        def copy(k):
            src_hbm, _, r0, c0, width = blocks[k]
            slot = k % n_slots
            return pltpu.make_async_copy(src_hbm.at[pl.ds(r0, stage_rows), pl.ds(c0, width)],
                                         stage_ref.at[slot, :, pl.ds(0, width)], stage_sem.at[slot])

        for k in range(min(n_slots - 1, len(blocks))):
            copy(k).start()
        while_in_flight()
        for k, (_, dst_ref, r0, c0, width) in enumerate(blocks):
            if k + n_slots - 1 < len(blocks):
                copy(k + n_slots - 1).start()
            copy(k).wait()
            dst_ref[r0:r0 + stage_rows, c0:c0 + width] = stage_ref[k % n_slots, :, 0:width].astype(BF16)

    def build_tables():
        x1_ref[...] = jnp.zeros(x1_ref.shape, F32)
        for k in range(CONV_WIDTH):
            cwb_ref[k] = jnp.broadcast_to(cw_ref[k:k + 1, :], (8, d_conv))
        n = (qi - kj) & (WINDOW - 1)
        max_exact = N_BUCKETS // 2
        nf = jnp.maximum(n, max_exact).astype(F32)
        large = max_exact + (jnp.log(nf / max_exact) / math.log(MAX_DISTANCE / max_exact)
                             * (N_BUCKETS - max_exact)).astype(jnp.int32)
        large = jnp.minimum(large, N_BUCKETS - 1)
        bucket = jnp.where(n < max_exact, n, large)
        for hd in range(N_HEADS):
            bias = jnp.zeros((WINDOW, WINDOW), F32)
            for bkt in range(N_BUCKETS):
                bias = jnp.where(bucket == bkt, tbl_ref[bkt * N_HEADS + hd], bias)
            bias_ref[hd] = bias

    @pl.when(g == 0)
    def _first_step():
        load_weights(((w_in_hbm, w_in_ref), (w_out_hbm, w_out_ref), (w_up_hbm, w_up_ref), (wd_hbm, wd_ref)),
                     while_in_flight=build_tables)

    @pl.when(i == 0)
    def _reset_mixer_history():
        kv_ref[:, 0:WINDOW, :] = jnp.zeros((8, WINDOW, d_kv), BF16)
        hbuf_ref[0:CONV_HALO, :] = jnp.zeros((CONV_HALO, d_conv), F32)

    @pl.when(fi == 0)
    def _reset_ffn_history():
        cg_carry_ref[...] = jnp.zeros(cg_carry_ref.shape, F32)
        cu_carry_ref[...] = jnp.zeros(cu_carry_ref.shape, F32)

    def mixer():
        x = x_ref[...]
        proj = jnp.dot(x.astype(BF16), w_in_ref[...], preferred_element_type=F32) + b_in_ref[...]
        q_end, k_end, v_end, a_end = d_attn, d_attn + d_kv, d_attn + 2 * d_kv, d_attn + 2 * d_kv + d_conv
        yield 0

        hbuf_ref[CONV_HALO:CONV_HALO + ts, :] = proj[:, v_end:a_end] * jax.nn.sigmoid(proj[:, a_end:])
        yield W_GLU
        for s in range(1, 8):
            hshift_ref[s - 1] = hbuf_ref[s:s + hshift_ref.shape[1], :]
            yield W_SHIFT

        lane = lax.broadcasted_iota(jnp.int32, (ts, d_kv), 1)
        lo = lane < HEAD_DIM
        for base, t in ((0, proj[:, q_end:k_end]), (4, proj[:, k_end:v_end])):
            tr = pltpu.roll(t, HEAD_DIM, axis=1)
            zero = jnp.zeros_like(t)
            kv_ref[base + 0, WINDOW:WINDOW + ts, :] = jnp.where(lo, t, zero).astype(BF16)
            kv_ref[base + 1, WINDOW:WINDOW + ts, :] = jnp.where(lo, zero, tr).astype(BF16)
            kv_ref[base + 2, WINDOW:WINDOW + ts, :] = jnp.where(lo, tr, zero).astype(BF16)
            kv_ref[base + 3, WINDOW:WINDOW + ts, :] = jnp.where(lo, zero, t).astype(BF16)
            yield W_KV
        qs = (proj[:, 0:q_end] * (HEAD_DIM ** -0.5)).astype(BF16)
        yield W_Q

        conv, attn = conv_pieces(), attention_pieces(qs)
        for w in conv:
            yield w
            yield next(conv, 0)
            yield next(attn, 0)
        for w in attn:
            yield w

        mix = jnp.dot(y_ref[...], w_out_ref[...], preferred_element_type=F32) + b_out_ref[...]
        x1_ref[...] = _layer_norm(alpha * x + mix, l1g_ref[...], l1b_ref[...])
        yield 0

    def attention_pieces(qs):
        lo_w = lax.broadcasted_iota(jnp.int32, (WINDOW, 2 * HEAD_DIM), 1) < HEAD_DIM
        units = [(j, h) for j in range(ts // WINDOW) for h in range(N_KV_HEADS)]
        scores, probs, inv, pair_out = {}, {}, {}, {}

        def score_dots(j, h):
            qb = qs[j * WINDOW:(j + 1) * WINDOW, :]
            q2 = jnp.concatenate([qb[:, (2 * h) * LANES:(2 * h + 1) * LANES],
                                  qb[:, (2 * h + 1) * LANES:(2 * h + 2) * LANES]], axis=0)
            scores[j, h] = [_dot_nt(q2, kv_ref[2 * h + ab, j * WINDOW:(j + 2) * WINDOW, :])
                            for ab in range(2)]

        def softmax(j, h):
            has_prev = (i > 0) | (j > 0)
            prev_mask = jnp.where(has_prev, 0.0, -jnp.inf).astype(F32)
            for ab in range(2):
                s = scores[j, h][ab]
                for pp in range(2):
                    hd = 4 * h + 2 * pp + ab
                    s_prev = s[pp * WINDOW:(pp + 1) * WINDOW, 0:WINDOW]
                    s_cur = s[pp * WINDOW:(pp + 1) * WINDOW, WINDOW:2 * WINDOW]
                    sc = jnp.where(tri, s_cur, s_prev + prev_mask) + bias_ref[hd]
                    sink = sinks_ref[hd]
                    m = jnp.maximum(jnp.max(sc, axis=-1, keepdims=True), sink)
                    p = jnp.exp(sc - m)
                    denom = jnp.sum(p, axis=-1, keepdims=True) + jnp.exp(sink - m)
                    zero = jnp.zeros_like(p)
                    probs[j, h, pp, ab] = jnp.concatenate(
                        [jnp.where(tri, zero, p), jnp.where(tri, p, zero)], axis=1).astype(BF16)
                    inv[j, h, pp, ab] = 1.0 / denom

        def weighted_values(j, h):
            rows = slice(j * WINDOW, (j + 2) * WINDOW)
            for pp in range(2):
                o = (jnp.dot(probs[j, h, pp, 0], kv_ref[4 + 2 * h + 0, rows, :], preferred_element_type=F32)
                     + jnp.dot(probs[j, h, pp, 1], kv_ref[4 + 2 * h + 1, rows, :], preferred_element_type=F32))
                pair_out[j, h, pp] = o * jnp.where(lo_w, inv[j, h, pp, 0], inv[j, h, pp, 1])
            if h + 1 == N_KV_HEADS:
                ya = jnp.concatenate([pair_out[j, hh, pp] for hh in range(N_KV_HEADS) for pp in range(2)], axis=1)
                y_ref[j * WINDOW:(j + 1) * WINDOW, 0:d_attn] = _rms_norm(ya, ag_ref[...]).astype(BF16)

        for t in range(len(units) + 2):
            if t >= 2:
                weighted_values(*units[t - 2])
            if 1 <= t <= len(units):
                softmax(*units[t - 1])
            if t < len(units):
                score_dots(*units[t])
            yield W_ATTN
        kv_ref[:, 0:WINDOW, :] = kv_ref[:, ts:ts + WINDOW, :]

    def conv_pieces():
        for c in range(ts // CONV_ROWS):
            r0 = c * CONV_ROWS
            acc = jnp.broadcast_to(cb_ref[...], (CONV_ROWS, d_conv))
            for k in range(CONV_WIDTH):
                off = CONV_HALO - (CONV_WIDTH - 1) + k
                s, aligned = off % 8, off - off % 8
                rows = slice(r0 + aligned, r0 + aligned + CONV_ROWS)
                tap = hbuf_ref[rows, :] if s == 0 else hshift_ref[s - 1, rows, :]
                acc = acc + jnp.concatenate([cwb_ref[k]] * (CONV_ROWS // 8), axis=0) * tap
            hn = _layer_norm(acc, clg_ref[...], clb_ref[...])
            y_ref[r0:r0 + CONV_ROWS, d_attn:d_attn + d_conv] = (
                _rms_norm(hn * jax.nn.sigmoid(hn), cg_ref[...]).astype(BF16))
            yield W_CONV
        hbuf_ref[0:CONV_HALO, :] = hbuf_ref[ts:ts + CONV_HALO, :]

    mixer_pieces = mixer()
    interleavable = (W_GLU + 7 * W_SHIFT + 2 * W_KV + W_Q + ((ts // WINDOW) * N_KV_HEADS + 2) * W_ATTN
                     + (ts // CONV_ROWS) * W_CONV)

    emitted = [0]

    def emit_mixer(target):
        while emitted[0] < target:
            emitted[0] += next(mixer_pieces)

    seg = ts // 8
    x1_prev = x1_ref[...]
    for l in range(n_slabs):
        for s in range(8):
            for u in range(seg // 8):
                n0 = s * seg + 8 * u
                perm_ref[l, pl.ds(64 * u + s, 8, stride=8), :] = x1_prev[n0:n0 + 8, l * LANES:(l + 1) * LANES]
    x1pb = jnp.concatenate([perm_ref[l] for l in range(n_slabs)], axis=1).astype(BF16)
    next(mixer_pieces)

    first_sublane = lax.broadcasted_iota(jnp.int32, (8, FFN_CHUNK), 0) == 0

    def causal_taps(up, carry_ref, c):
        prev = carry_ref[c]
        carry_ref[c] = up[ts - 16:ts]
        roll1 = lambda t: pltpu.roll(t, 1, axis=0)
        head1 = jnp.where(first_sublane, roll1(prev[8:16]), roll1(up[ts - 8:ts]))
        head2 = jnp.where(first_sublane, roll1(prev[0:8]), roll1(up[ts - 16:ts - 8]))
        return (jnp.concatenate([head1, up[0:ts - 8]], axis=0),
                jnp.concatenate([head2, head1, up[0:ts - 16]], axis=0))

    def neg_conv3(col0, carry_ref, c):
        cols = slice(col0 + c * FFN_CHUNK, col0 + (c + 1) * FFN_CHUNK)
        up = jnp.dot(x1pb, w_up_ref[:, cols], preferred_element_type=F32)
        up1, up2 = causal_taps(up, carry_ref, c)
        w0, w1, w2 = -fw_ref[0:1, cols], -fw_ref[1:2, cols], -fw_ref[2:3, cols]
        return w2 * up + (w1 * up1 + (w0 * up2 - fb_ref[:, cols]))

    split = DOWN_SPLIT_CHUNKS * FFN_CHUNK
    ffn = None
    for c in range(n_chunks):
        ng = neg_conv3(0, cg_carry_ref, c)
        nu = neg_conv3(d_ff, cu_carry_ref, c)
        act = ng * nu * (1.0 / (1.0 + jnp.exp(ng)))
        act_ref[:, c * FFN_CHUNK:(c + 1) * FFN_CHUNK] = act.astype(BF16)
        emit_mixer(interleavable * (c + 1) // n_chunks)
        if (c + 1) * FFN_CHUNK == split:
            ffn = jnp.dot(act_ref[:, 0:split], wd_ref[0:split, :], preferred_element_type=F32)
    ffn = ffn + jnp.dot(act_ref[:, split:], wd_ref[split:, :], preferred_element_type=F32)
    for _ in mixer_pieces:
        pass

    x1p = jnp.concatenate([perm_ref[l] for l in range(n_slabs)], axis=1)
    out_p = _layer_norm(alpha * x1p + ffn, l2g_ref[...], l2b_ref[...])

    pitch = nat_ref.shape[1] // 8
    for l in range(n_slabs):
        for v in range(seg):
            nat_ref[l, pl.ds(v, 8, stride=pitch), :] = out_p[v * 8:(v + 1) * 8, l * LANES:(l + 1) * LANES]
    for l in range(n_slabs):
        for s in range(8):
            o_ref[s * seg:(s + 1) * seg, l * LANES:(l + 1) * LANES] = nat_ref[l, s * pitch:s * pitch + seg, :]


def _full_spec(shape):
    return pl.BlockSpec(shape, lambda g: (0,) * len(shape))


def _layer(x, alpha, w_in, b_in, sinks, tbl, cw, cb, clg, clb, ag, cg, w_out, b_out, l1g, l1b,
           w_up, fw, fb, w_down, l2g, l2b, seq_tile):
    bsz, seq, d_model = x.shape
    d_conv = cw.shape[1]
    d_ff = w_down.shape[0]
    d_kv = N_KV_HEADS * HEAD_DIM
    n_chunks = d_ff // FFN_CHUNK
    tiles_per_row = seq // seq_tile
    n_tiles = bsz * tiles_per_row
    seg = seq_tile // 8
    nat_pitch = seg + 8
    assert seq % seq_tile == 0 and seq_tile % WINDOW == 0 and d_ff % FFN_CHUNK == 0
    assert seg % 16 == 0 and d_model % LANES == 0

    row = lambda v: v.reshape(1, -1).astype(F32)
    weights = [w.astype(F32) for w in (w_in, w_out, w_up, w_down)]
    assert all(w.shape[0] % STAGE_ROWS == 0 and w.shape[1] % LANES == 0 for w in weights)
    args = [
        x,
        weights[0], row(b_in), sinks.astype(F32), tbl.reshape(-1).astype(F32),
        cw.astype(F32), row(cb), row(clg), row(clb), row(ag), row(cg),
        weights[1], row(b_out), row(l1g), row(l1b),
        weights[2], fw.astype(F32), row(fb), weights[3], row(l2g), row(l2b),
    ]
    smem = pl.BlockSpec(memory_space=pltpu.SMEM)
    in_hbm = pl.BlockSpec(memory_space=pl.ANY)
    big = {1, 11, 15, 18}

    def mixer_tile(g):
        t = jnp.minimum(g, n_tiles - 1)
        return (t // tiles_per_row, t % tiles_per_row, 0)

    def ffn_tile(g):
        t = jnp.maximum(g - 1, 0)
        return (t // tiles_per_row, t % tiles_per_row, 0)

    in_specs = [pl.BlockSpec((None, seq_tile, d_model), mixer_tile)]
    for idx, a in enumerate(args[1:], start=1):
        in_specs.append(smem if idx in (3, 4) else in_hbm if idx in big else _full_spec(a.shape))

    scratch = [pltpu.VMEM(w.shape, BF16) for w in weights] + [
        pltpu.VMEM((STAGE_SLOTS, STAGE_ROWS, STAGE_COLS), F32),
        pltpu.SemaphoreType.DMA((STAGE_SLOTS,)),
        pltpu.VMEM((N_HEADS, WINDOW, WINDOW), F32),
        pltpu.VMEM((CONV_WIDTH, 8, d_conv), F32),
        pltpu.VMEM((8, seq_tile + WINDOW, d_kv), BF16),
        pltpu.VMEM((seq_tile + CONV_HALO, d_conv), F32),
        pltpu.VMEM((7, seq_tile + CONV_HALO - 8, d_conv), F32),
        pltpu.VMEM((seq_tile, d_model), BF16),
        pltpu.VMEM((seq_tile, d_model), F32),
        pltpu.VMEM((d_model // LANES, seq_tile, LANES), F32),
        pltpu.VMEM((d_model // LANES, 8 * nat_pitch, LANES), F32),
        pltpu.VMEM((seq_tile, d_ff), BF16),
        pltpu.VMEM((n_chunks, 16, FFN_CHUNK), F32),
        pltpu.VMEM((n_chunks, 16, FFN_CHUNK), F32),
    ]
    return pl.pallas_call(
        functools.partial(_layer_kernel, alpha, n_tiles, tiles_per_row),
        grid=(n_tiles + 1,),
        in_specs=in_specs,
        out_specs=pl.BlockSpec((None, seq_tile, d_model), ffn_tile),
        out_shape=jax.ShapeDtypeStruct(x.shape, x.dtype),
        scratch_shapes=scratch,
        compiler_params=pltpu.CompilerParams(
            dimension_semantics=("arbitrary",),
            vmem_limit_bytes=VMEM_LIMIT_BYTES),
        name="hybrid_layer",
    )(*args)


def kernel(x, w_in, b_in, attn_sinks, rel_bias_table, conv_dw_w, conv_dw_b, conv_ln_g, conv_ln_b,
           attn_out_gain, conv_out_gain, w_out, b_out, ln1_g, ln1_b, w_up, ffn_dw_w, ffn_dw_b, w_down,
           ln2_g, ln2_b, seq_tile=SEQ_TILE):
    depth = w_in.shape[0]
    alpha = (2.0 * depth) ** 0.25
    for l in range(depth):
        x = _layer(x, alpha, w_in[l], b_in[l], attn_sinks[l], rel_bias_table, conv_dw_w[l], conv_dw_b[l],
                   conv_ln_g[l], conv_ln_b[l], attn_out_gain[l], conv_out_gain[l], w_out[l], b_out[l],
                   ln1_g[l], ln1_b[l], w_up[l], ffn_dw_w[l], ffn_dw_b[l], w_down[l], ln2_g[l], ln2_b[l],
                   seq_tile)
    return x
```
